```python
import jax, jax.numpy as jnp
from jax import lax
import numpy as np

D_MODEL = 2048
BATCH = 2
SEQ = 8192
DEPTH = 1

D_RNN = D_MODEL
RNN_BLOCKS = 16
RNN_BLOCK_DIM = D_RNN // RNN_BLOCKS
CONV_WIDTH = 4
LRU_C = 8.0
HEAD_DIM = 128
HEADS_PER_GROUP = 4
DILATION_GROUPS = ((128, 1), (512, 4), (2048, 16))
N_GROUPS = len(DILATION_GROUPS)
ATT_HEADS = N_GROUPS * HEADS_PER_GROUP
ATT_WIDTH = ATT_HEADS * HEAD_DIM
ATT_OUT = HEADS_PER_GROUP * HEAD_DIM
ROPE_THETA = 10000.0
IN_WIDTH = 2 * D_RNN + 3 * ATT_WIDTH + 2 * D_MODEL
N_EXPERTS = 256
TOP_K = 8
N_EXPERT_GROUPS = 8
TOPK_GROUPS = 4
D_EXPERT = 512
ROUTED_SCALE = 2.5
MOE_BLOCK = 128
NORM_EPS = 1e-6

kernel_name = 'hybrid_rglru_dilated_attn_moe_block'


def rms_norm(x, g):
    x32 = x.astype(jnp.float32)
    y = x32 * lax.rsqrt(jnp.mean(x32 * x32, axis=-1, keepdims=True) + NORM_EPS)
    return (y * g.astype(jnp.float32)).astype(x.dtype)


def modulate(h, shift, scale):
    return h * (1.0 + scale[:, None, :]) + shift[:, None, :]


def causal_depthwise_conv(u, w, b):
    s = u.shape[1]
    up = jnp.pad(u, ((0, 0), (CONV_WIDTH - 1, 0), (0, 0)))
    out = b
    for j in range(CONV_WIDTH):
        out = out + w[j] * up[:, j:j + s]
    return out


def rg_lru(u, wa, ba, wi, bi, lam):
    bsz, s, _ = u.shape
    ub = u.reshape(bsz, s, RNN_BLOCKS, RNN_BLOCK_DIM)
    r = jax.nn.sigmoid(jnp.einsum('bsnc,ncd->bsnd', ub, wa) + ba).reshape(bsz, s, D_RNN)
    i = jax.nn.sigmoid(jnp.einsum('bsnc,ncd->bsnd', ub, wi) + bi).reshape(bsz, s, D_RNN)
    log_a = -LRU_C * r.astype(jnp.float32) * jax.nn.softplus(-lam.astype(jnp.float32))
    a = jnp.exp(log_a)
    gated_x = jnp.sqrt(-jnp.expm1(2.0 * log_a)) * (i * u).astype(jnp.float32)

    def combine(left, right):
        a_l, b_l = left
        a_r, b_r = right
        return a_l * a_r, a_r * b_l + b_r

    _, h = lax.associative_scan(combine, (a, gated_x), axis=1)
    return h.astype(u.dtype)


def apply_rope(t, positions):
    half = HEAD_DIM // 2
    inv_freq = ROPE_THETA ** (-jnp.arange(half, dtype=jnp.float32) * 2.0 / HEAD_DIM)
    ang = positions.astype(jnp.float32)[..., None] * inv_freq
    cos = jnp.cos(ang)[:, :, None, :]
    sin = jnp.sin(ang)[:, :, None, :]
    t32 = t.astype(jnp.float32)
    t1, t2 = t32[..., :half], t32[..., half:]
    return jnp.concatenate([t1 * cos - t2 * sin, t2 * cos + t1 * sin], axis=-1).astype(t.dtype)


def dilated_window_attention(q, k, v, window, dilation):
    bsz, s, h, dh = q.shape
    blk = window // dilation
    length = s // dilation
    nb = -(-length // blk)
    lp = nb * blk

    def to_sub(t):
        t = t.reshape(bsz, length, dilation, h, dh)
        return jnp.pad(t, ((0, 0), (0, lp - length), (0, 0), (0, 0), (0, 0)))

    def kv_blocks(t):
        t = jnp.pad(to_sub(t), ((0, 0), (blk, 0), (0, 0), (0, 0), (0, 0)))
        t = t.reshape(bsz, nb + 1, blk, dilation, h, dh)
        return jnp.concatenate([t[:, :-1], t[:, 1:]], axis=2)

    qs = to_sub(q).reshape(bsz, nb, blk, dilation, h, dh)
    ks = kv_blocks(k)
    vs = kv_blocks(v)
    scores = jnp.einsum('bnqrhd,bnkrhd->bnrhqk', qs, ks).astype(jnp.float32)
    qi = jnp.arange(blk)[:, None]
    kj = jnp.arange(2 * blk)[None, :]
    dist = blk + qi - kj
    band = (dist >= 0) & (dist <= blk)
    has_prev = (jnp.arange(nb)[:, None] > 0) | (kj >= blk)
    mask = band[None] & has_prev[:, None, :]
    scores = jnp.where(mask[None, :, None, None], scores, -jnp.inf)
    m = jnp.max(scores, axis=-1, keepdims=True)
    p = jnp.exp(scores - m)
    den = jnp.sum(p, axis=-1, keepdims=True)
    out = jnp.einsum('bnrhqk,bnkrhd->bnqrhd', (p / den).astype(v.dtype), vs)
    lse = (m + jnp.log(den))[..., 0]
    out = out.reshape(bsz, lp, dilation, h, dh)[:, :length].reshape(bsz, s, h, dh)
    lse = jnp.transpose(lse, (0, 1, 4, 2, 3)).reshape(bsz, lp, dilation, h)[:, :length].reshape(bsz, s, h)
    return out, lse


def mixture_of_dilations(q, k, v):
    outs, lses = [], []
    for g, (window, dilation) in enumerate(DILATION_GROUPS):
        hs = slice(g * HEADS_PER_GROUP, (g + 1) * HEADS_PER_GROUP)
        o, l = dilated_window_attention(q[:, :, hs], k[:, :, hs], v[:, :, hs], window, dilation)
        outs.append(o)
        lses.append(l)
    outs = jnp.stack(outs)
    lses = jnp.stack(lses)
    wts = jax.nn.softmax(lses, axis=0)
    return jnp.einsum('gbsh,gbshd->bshd', wts.astype(outs.dtype), outs)


def swiglu(x, w1, w3, w2):
    return (jax.nn.silu(x @ w1) * (x @ w3)) @ w2


def route(h_tok, router_w, router_bias):
    t = h_tok.shape[0]
    scores = jax.nn.sigmoid((h_tok @ router_w).astype(jnp.float32))
    sel = scores + router_bias.astype(jnp.float32)
    grp = sel.reshape(t, N_EXPERT_GROUPS, N_EXPERTS // N_EXPERT_GROUPS)
    grp_score = jnp.sum(lax.top_k(grp, 2)[0], axis=-1)
    _, gidx = lax.top_k(grp_score, TOPK_GROUPS)
    gmask = jnp.any(gidx[:, :, None] == jnp.arange(N_EXPERT_GROUPS)[None, None, :], axis=1)
    emask = jnp.repeat(gmask, N_EXPERTS // N_EXPERT_GROUPS, axis=1)
    _, idx = lax.top_k(jnp.where(emask, sel, -jnp.inf), TOP_K)
    w = jnp.take_along_axis(scores, idx, axis=1)
    w = w / jnp.sum(w, axis=-1, keepdims=True) * ROUTED_SCALE
    return idx.astype(jnp.int32), w


def routed_experts(h_tok, idx, wts, w1, w3, w2, layer):
    t, d = h_tok.shape
    n_assign = t * TOP_K
    n_pad = -(-n_assign // MOE_BLOCK) * MOE_BLOCK + N_EXPERTS * MOE_BLOCK
    n_blocks = n_pad // MOE_BLOCK
    e_flat = idx.reshape(-1)
    w_flat = wts.reshape(-1).astype(jnp.float32)
    tok_flat = jnp.repeat(jnp.arange(t, dtype=jnp.int32), TOP_K)
    order = jnp.argsort(e_flat, stable=True)
    e_sorted = e_flat[order]
    counts = jnp.bincount(e_flat, length=N_EXPERTS).astype(jnp.int32)
    starts = jnp.cumsum(counts) - counts
    padded = (counts + MOE_BLOCK - 1) // MOE_BLOCK * MOE_BLOCK
    pends = jnp.cumsum(padded)
    pstarts = pends - padded
    dest = pstarts[e_sorted] + jnp.arange(n_assign, dtype=jnp.int32) - starts[e_sorted]
    row_tok = jnp.full((n_pad,), t, jnp.int32).at[dest].set(tok_flat[order])
    row_w = jnp.zeros((n_pad,), jnp.float32).at[dest].set(w_flat[order])
    block_start = jnp.arange(n_blocks, dtype=jnp.int32) * MOE_BLOCK
    block_expert = jnp.minimum(jnp.searchsorted(pends, block_start, side='right'), N_EXPERTS - 1).astype(jnp.int32)
    x_ext = jnp.concatenate([h_tok, jnp.zeros((1, d), h_tok.dtype)], axis=0)

    def body(acc, blk):
        e, rows, rw = blk
        xb = x_ext[rows]
        y = swiglu(xb, w1[layer, e], w3[layer, e], w2[layer, e]).astype(jnp.float32)
        return acc.at[rows].add(y * rw[:, None]), None

    acc0 = jnp.zeros((t + 1, d), jnp.float32)
    acc, _ = lax.scan(body, acc0, (block_expert, row_tok.reshape(n_blocks, MOE_BLOCK), row_w.reshape(n_blocks, MOE_BLOCK)))
    return acc[:t].astype(h_tok.dtype)


def setup_inputs(seed: int = 0) -> dict:
    key = jax.random.key(seed)
    ks = jax.random.split(key, 32)

    def nrm(k, shape, scale):
        return jax.random.normal(k, shape, jnp.float32) * scale

    def gain(k, shape):
        return 1.0 + 0.01 * jax.random.normal(k, shape, jnp.float32)

    u = jax.random.uniform(ks[10], (DEPTH, D_RNN), jnp.float32, 0.9, 0.999)
    a0 = u ** (1.0 / LRU_C)
    offsets = jax.random.randint(ks[2], (BATCH, 1), 0, 4096, jnp.int32)
    return {
        'x': nrm(ks[0], (BATCH, SEQ, D_MODEL), 1.0),
        'c': nrm(ks[1], (BATCH, D_MODEL), 1.0),
        'positions': offsets + jnp.arange(SEQ, dtype=jnp.int32)[None, :],
        'ada_w': nrm(ks[3], (DEPTH, D_MODEL, 6 * D_MODEL), 0.5 * D_MODEL ** -0.5),
        'ada_b': nrm(ks[4], (DEPTH, 6 * D_MODEL), 0.01),
        'norm1_g': gain(ks[5], (DEPTH, D_MODEL)),
        'w_in': nrm(ks[6], (DEPTH, D_MODEL, IN_WIDTH), D_MODEL ** -0.5),
        'conv_w': nrm(ks[7], (DEPTH, CONV_WIDTH, D_RNN), CONV_WIDTH ** -0.5),
        'conv_b': nrm(ks[8], (DEPTH, D_RNN), 0.01),
        'rg_wa': nrm(ks[9], (DEPTH, RNN_BLOCKS, RNN_BLOCK_DIM, RNN_BLOCK_DIM), RNN_BLOCK_DIM ** -0.5),
        'rg_ba': nrm(ks[11], (DEPTH, RNN_BLOCKS, RNN_BLOCK_DIM), 0.01),
        'rg_wi': nrm(ks[12], (DEPTH, RNN_BLOCKS, RNN_BLOCK_DIM, RNN_BLOCK_DIM), RNN_BLOCK_DIM ** -0.5),
        'rg_bi': nrm(ks[13], (DEPTH, RNN_BLOCKS, RNN_BLOCK_DIM), 0.01),
        'rg_lambda': jnp.log(a0) - jnp.log1p(-a0),
        'w_proj_rnn': nrm(ks[14], (DEPTH, D_RNN, D_MODEL), D_RNN ** -0.5),
        'w_proj_attn': nrm(ks[15], (DEPTH, ATT_OUT, D_MODEL), ATT_OUT ** -0.5),
        'w_out': nrm(ks[16], (DEPTH, D_MODEL, D_MODEL), D_MODEL ** -0.5),
        'norm2_g': gain(ks[17], (DEPTH, D_MODEL)),
        'router_w': nrm(ks[18], (DEPTH, D_MODEL, N_EXPERTS), D_MODEL ** -0.5),
        'router_bias': nrm(ks[19], (DEPTH, N_EXPERTS), 0.01),
        'exp_w1': nrm(ks[20], (DEPTH, N_EXPERTS, D_MODEL, D_EXPERT), D_MODEL ** -0.5),
        'exp_w3': nrm(ks[21], (DEPTH, N_EXPERTS, D_MODEL, D_EXPERT), D_MODEL ** -0.5),
        'exp_w2': nrm(ks[22], (DEPTH, N_EXPERTS, D_EXPERT, D_MODEL), D_EXPERT ** -0.5),
        'sh_w1': nrm(ks[23], (DEPTH, D_MODEL, D_EXPERT), D_MODEL ** -0.5),
        'sh_w3': nrm(ks[24], (DEPTH, D_MODEL, D_EXPERT), D_MODEL ** -0.5),
        'sh_w2': nrm(ks[25], (DEPTH, D_EXPERT, D_MODEL), D_EXPERT ** -0.5),
        'final_g': gain(ks[26], (D_MODEL,)),
    }


def reference(x, c, positions, ada_w, ada_b, norm1_g, w_in, conv_w, conv_b, rg_wa, rg_ba, rg_wi, rg_bi,
              rg_lambda, w_proj_rnn, w_proj_attn, w_out, norm2_g, router_w, router_bias,
              exp_w1, exp_w3, exp_w2, sh_w1, sh_w3, sh_w2, final_g):
    bsz, s, d = x.shape
    splits = [D_RNN, 2 * D_RNN, 2 * D_RNN + ATT_WIDTH, 2 * D_RNN + 2 * ATT_WIDTH,
              2 * D_RNN + 3 * ATT_WIDTH, 2 * D_RNN + 3 * ATT_WIDTH + D_MODEL]
    for l in range(DEPTH):
        mod = jax.nn.silu(c) @ ada_w[l] + ada_b[l]
        shift1, scale1, gate1, shift2, scale2, gate2 = jnp.split(mod, 6, axis=-1)

        h = modulate(rms_norm(x, norm1_g[l]), shift1, scale1)
        proj = h @ w_in[l]
        xr, gr, q, k, v, ga, gb = jnp.split(proj, splits, axis=-1)
        u = causal_depthwise_conv(xr, conv_w[l], conv_b[l])
        ya = rg_lru(u, rg_wa[l], rg_ba[l], rg_wi[l], rg_bi[l], rg_lambda[l]) * jax.nn.gelu(gr)
        q = apply_rope(q.reshape(bsz, s, ATT_HEADS, HEAD_DIM), positions) * (HEAD_DIM ** -0.5)
        k = apply_rope(k.reshape(bsz, s, ATT_HEADS, HEAD_DIM), positions)
        v = v.reshape(bsz, s, ATT_HEADS, HEAD_DIM)
        yb = mixture_of_dilations(q, k, v).reshape(bsz, s, ATT_OUT)
        merged = jax.nn.sigmoid(ga) * (ya @ w_proj_rnn[l]) + jax.nn.sigmoid(gb) * (yb @ w_proj_attn[l])
        x = x + gate1[:, None, :] * (merged @ w_out[l])

        h = modulate(rms_norm(x, norm2_g[l]), shift2, scale2)
        h_tok = h.reshape(bsz * s, d)
        idx, wts = route(h_tok, router_w[l], router_bias[l])
        y = routed_experts(h_tok, idx, wts, exp_w1, exp_w3, exp_w2, l) + swiglu(h_tok, sh_w1[l], sh_w3[l], sh_w2[l])
        x = x + gate2[:, None, :] * y.reshape(bsz, s, d)
    return rms_norm(x, final_g)
```

```python
import functools
import math

import jax
import jax.numpy as jnp
from jax import lax
from jax.experimental import pallas as pl
from jax.experimental.pallas import tpu as pltpu

F32 = jnp.float32
BF16 = jnp.bfloat16
I32 = jnp.int32

HEAD_DIM = 128
HEADS_PER_GROUP = 4
DILATION_GROUPS = ((128, 1), (512, 4), (2048, 16))
ROPE_THETA = 10000.0
CONV_WIDTH = 4
LRU_C = 8.0
TOP_K = 8
N_EXPERT_GROUPS = 8
TOPK_GROUPS = 4
ROUTED_SCALE = 2.5
NORM_EPS = 1e-6

LANES = 128
SUBLANES = 8
VMEM_LIMIT_BYTES = 56 * 1024 * 1024

GROUP_COLS = HEADS_PER_GROUP * HEAD_DIM
EXPERT_ROWS = 256
DEST_TOKENS = 128


def _params(*sem):
    return pltpu.CompilerParams(dimension_semantics=sem, vmem_limit_bytes=VMEM_LIMIT_BYTES)


def _gelu_tanh(x):
    return 0.5 * x * (1.0 + jnp.tanh(math.sqrt(2.0 / math.pi) * (x + 0.044715 * (x * x * x))))


def _silu(x):
    return x * jax.nn.sigmoid(x)


def _rms(x, g):
    ms = jnp.mean(x * x, axis=-1, keepdims=True)
    return x * lax.rsqrt(ms + NORM_EPS) * g


def _ada_kernel(c_ref, w_ref, b_ref, o_ref):
    a = _silu(c_ref[...]).astype(BF16)
    o_ref[...] = jnp.dot(a, w_ref[...].astype(BF16), preferred_element_type=F32) + b_ref[...]


def _ada_mod(c_pad, ada_w, ada_b):
    rows, d = c_pad.shape
    n = ada_w.shape[1]
    tn = 1024
    return pl.pallas_call(
        _ada_kernel,
        grid=(n // tn,),
        in_specs=[
            pl.BlockSpec((rows, d), lambda j: (0, 0)),
            pl.BlockSpec((d, tn), lambda j: (0, j)),
            pl.BlockSpec((1, tn), lambda j: (0, j)),
        ],
        out_specs=pl.BlockSpec((rows, tn), lambda j: (0, j)),
        out_shape=jax.ShapeDtypeStruct((rows, n), F32),
        compiler_params=_params("arbitrary"),
        name="ada_mod",
    )(c_pad, ada_w, ada_b)


def _rope_kernel(pos_ref, freq_ref, sign_ref, cos_ref, sin_ref):
    ang = pos_ref[...].astype(F32) * freq_ref[...]
    cos_ref[...] = jnp.cos(ang)
    sin_ref[...] = jnp.sin(ang) * sign_ref[...]


def _rope_tables(pos_col, freq, sign):
    t = pos_col.shape[0]
    tm = 1024
    return pl.pallas_call(
        _rope_kernel,
        grid=(t // tm,),
        in_specs=[
            pl.BlockSpec((tm, 1), lambda i: (i, 0)),
            pl.BlockSpec((1, HEAD_DIM), lambda i: (0, 0)),
            pl.BlockSpec((1, HEAD_DIM), lambda i: (0, 0)),
        ],
        out_specs=[pl.BlockSpec((tm, HEAD_DIM), lambda i: (i, 0))] * 2,
        out_shape=[jax.ShapeDtypeStruct((t, HEAD_DIM), F32)] * 2,
        compiler_params=_params("arbitrary"),
        name="rope_tables",
    )(pos_col, freq, sign)


def _norm_kernel(x_ref, g_ref, sh_ref, sc_ref, h_ref):
    h_ref[...] = (_rms(x_ref[...], g_ref[...]) * (1.0 + sc_ref[...]) + sh_ref[...]).astype(h_ref.dtype)


def _norm_mod(x2, g, shift, scale, seq):
    t, d = x2.shape
    tm = 1024
    per_b = seq // tm
    bvec = pl.BlockSpec((None, 1, d), lambda i: (i // per_b, 0, 0))
    return pl.pallas_call(
        _norm_kernel,
        grid=(t // tm,),
        in_specs=[pl.BlockSpec((tm, d), lambda i: (i, 0)), pl.BlockSpec((1, d), lambda i: (0, 0)), bvec, bvec],
        out_specs=pl.BlockSpec((tm, d), lambda i: (i, 0)),
        out_shape=jax.ShapeDtypeStruct((t, d), BF16),
        compiler_params=_params("arbitrary"),
        name="norm_mod",
    )(x2, g, shift, scale)


def _proj_act_kernel(h_ref, w_ref, o_ref, *, plain_tiles, act):
    acc = jnp.dot(h_ref[...], w_ref[...], preferred_element_type=F32)
    j = pl.program_id(1)

    @pl.when(j < plain_tiles)
    def _():
        o_ref[...] = acc.astype(o_ref.dtype)

    @pl.when(j >= plain_tiles)
    def _():
        o_ref[...] = act(acc).astype(o_ref.dtype)


def _proj_act(h1, w_in_b, col0, ncols, plain_tiles, act, name):
    t, d = h1.shape
    tm, tn = 1024, GROUP_COLS
    j0 = col0 // tn
    return pl.pallas_call(
        functools.partial(_proj_act_kernel, plain_tiles=plain_tiles, act=act),
        grid=(t // tm, ncols // tn),
        in_specs=[pl.BlockSpec((tm, d), lambda i, j: (i, 0)),
                  pl.BlockSpec((d, tn), lambda i, j: (0, j0 + j))],
        out_specs=pl.BlockSpec((tm, tn), lambda i, j: (i, j)),
        out_shape=jax.ShapeDtypeStruct((t, ncols), BF16),
        compiler_params=_params("arbitrary", "arbitrary"),
        name=name,
    )(h1, w_in_b)


def _proj_qkv_kernel(h_ref, w_ref, cos_ref, sin_ref, o_ref, scr, *, dilation):
    tm = h_ref.shape[0]
    acc = jnp.dot(h_ref[...], w_ref[...], preferred_element_type=F32)
    j = pl.program_id(1)

    @pl.when(j < 2)
    def _():
        scale = jnp.where(j == 0, HEAD_DIM ** -0.5, 1.0).astype(F32)
        c = cos_ref[...] * scale
        s = sin_ref[...] * scale
        for h in range(HEADS_PER_GROUP):
            v = acc[:, h * HEAD_DIM:(h + 1) * HEAD_DIM]
            scr[h] = v * c + pltpu.roll(v, HEAD_DIM // 2, axis=1) * s

    @pl.when(j == 2)
    def _():
        for h in range(HEADS_PER_GROUP):
            scr[h] = acc[:, h * HEAD_DIM:(h + 1) * HEAD_DIM]

    sub = tm // dilation
    for r in range(dilation):
        for h in range(HEADS_PER_GROUP):
            rows = scr[h] if dilation == 1 else scr[h, pl.ds(r, sub, stride=dilation), :]
            o_ref[r, :, h * HEAD_DIM:(h + 1) * HEAD_DIM] = rows.astype(o_ref.dtype)


def _proj_qkv(h1, w_in_b, cos_t, sin_t, g, dilation, bsz, seq, q_col, att_width):
    t, d = h1.shape
    tm, tn = 1024, GROUP_COLS
    per_b = seq // tm
    j0 = q_col // tn + g
    step = att_width // tn
    sub = tm // dilation
    return pl.pallas_call(
        functools.partial(_proj_qkv_kernel, dilation=dilation),
        grid=(t // tm, 3),
        in_specs=[pl.BlockSpec((tm, d), lambda i, j: (i, 0)),
                  pl.BlockSpec((d, tn), lambda i, j: (0, j0 + step * j)),
                  pl.BlockSpec((tm, HEAD_DIM), lambda i, j: (i, 0)),
                  pl.BlockSpec((tm, HEAD_DIM), lambda i, j: (i, 0))],
        out_specs=pl.BlockSpec((None, dilation, sub, tn), lambda i, j: (i // per_b, 0, i % per_b, j)),
        out_shape=jax.ShapeDtypeStruct((bsz, dilation, seq // dilation, 3 * tn), BF16),
        scratch_shapes=[pltpu.VMEM((HEADS_PER_GROUP, tm, HEAD_DIM), F32)],
        compiler_params=_params("arbitrary", "arbitrary"),
        name=f"proj_qkv_g{g}",
    )(h1, w_in_b, cos_t, sin_t)


def _rglru_kernel(xr_ref, gr_ref, cw_ref, cb_ref, wa_ref, ba_ref, wi_ref, bi_ref, lam_ref,
                  ya_ref, xbuf, hcar):
    tt = xr_ref.shape[0]
    halo = SUBLANES

    @pl.when(pl.program_id(2) == 0)
    def _():
        xbuf[0:halo, :] = jnp.zeros((halo, LANES), F32)
        hcar[...] = jnp.zeros_like(hcar)

    xbuf[halo:halo + tt, :] = xr_ref[...].astype(F32)
    u = cb_ref[...] + cw_ref[CONV_WIDTH - 1:CONV_WIDTH, :] * xbuf[halo:halo + tt, :]
    for j in range(CONV_WIDTH - 1):
        s = CONV_WIDTH - 1 - j
        u = u + cw_ref[j:j + 1, :] * xbuf[halo - s:halo - s + tt, :]
    xbuf[0:halo, :] = xbuf[tt:tt + halo, :]

    ub = u.astype(BF16)
    r = jax.nn.sigmoid(jnp.dot(ub, wa_ref[...], preferred_element_type=F32) + ba_ref[...])
    ig = jax.nn.sigmoid(jnp.dot(ub, wi_ref[...], preferred_element_type=F32) + bi_ref[...])
    z = -lam_ref[...]
    softplus = jnp.maximum(z, 0.0) + jnp.log1p(jnp.exp(-jnp.abs(z)))
    log_a = (-LRU_C) * r * softplus
    a = jnp.exp(log_a)
    th = jnp.tanh(log_a)
    b = jnp.sqrt(-2.0 * th / (1.0 - th)) * (ig * u)

    row = lax.broadcasted_iota(I32, (tt, LANES), 0)
    sh = 1
    while sh < tt:
        keep = row >= sh
        b = jnp.where(keep, a * pltpu.roll(b, sh, axis=0) + b, b)
        a = jnp.where(keep, a * pltpu.roll(a, sh, axis=0), a)
        sh *= 2
    h = b + a * hcar[0:1, :]
    hcar[...] = jnp.broadcast_to(h[tt - 1:tt, :], hcar.shape)
    ya_ref[...] = (h * gr_ref[...].astype(F32)).astype(ya_ref.dtype)


def _rglru(rnn, conv_w, conv_b, wa_b, ba, wi_b, bi, lam, bsz, seq, d_rnn):
    t = rnn.shape[0]
    tt = 512
    nct = d_rnn // LANES
    per_b = seq // tt
    row = lambda b, c, s: (b * per_b + s, c)
    vec = lambda b, c, s: (0, c)
    return pl.pallas_call(
        _rglru_kernel,
        grid=(bsz, nct, per_b),
        in_specs=[
            pl.BlockSpec((tt, LANES), row),
            pl.BlockSpec((tt, LANES), lambda b, c, s: (b * per_b + s, nct + c)),
            pl.BlockSpec((CONV_WIDTH, LANES), vec),
            pl.BlockSpec((1, LANES), vec),
            pl.BlockSpec((None, LANES, LANES), lambda b, c, s: (c, 0, 0)),
            pl.BlockSpec((1, LANES), vec),
            pl.BlockSpec((None, LANES, LANES), lambda b, c, s: (c, 0, 0)),
            pl.BlockSpec((1, LANES), vec),
            pl.BlockSpec((1, LANES), vec),
        ],
        out_specs=pl.BlockSpec((tt, LANES), row),
        out_shape=jax.ShapeDtypeStruct((t, d_rnn), BF16),
        scratch_shapes=[pltpu.VMEM((tt + SUBLANES, LANES), F32), pltpu.VMEM((SUBLANES, LANES), F32)],
        compiler_params=_params("arbitrary", "arbitrary", "arbitrary"),
        name="rglru",
    )(rnn, rnn, conv_w, conv_b, wa_b, ba, wi_b, bi, lam)


def _attn_kernel(q_ref, kc_ref, kp_ref, vc_ref, vp_ref, o_ref, l_ref):
    blk = q_ref.shape[0]
    has_prev = pl.program_id(2) > 0
    qi = lax.broadcasted_iota(I32, (blk, blk), 0)
    kj = lax.broadcasted_iota(I32, (blk, blk), 1)
    mask_prev = jnp.logical_and(kj >= qi, has_prev)
    mask_cur = kj <= qi
    nt = (((1,), (1,)), ((), ()))
    for h in range(HEADS_PER_GROUP):
        cs = slice(h * HEAD_DIM, (h + 1) * HEAD_DIM)
        q = q_ref[:, cs]
        sp = lax.dot_general(q, kp_ref[:, cs], nt, preferred_element_type=F32)
        sc = lax.dot_general(q, kc_ref[:, cs], nt, preferred_element_type=F32)
        sp = jnp.where(mask_prev, sp, -jnp.inf)
        sc = jnp.where(mask_cur, sc, -jnp.inf)
        m = jnp.maximum(jnp.max(sp, axis=-1, keepdims=True), jnp.max(sc, axis=-1, keepdims=True))
        pp = jnp.exp(sp - m)
        pc = jnp.exp(sc - m)
        den = jnp.sum(pp, axis=-1, keepdims=True) + jnp.sum(pc, axis=-1, keepdims=True)
        out = (jnp.dot((pp / den).astype(BF16), vp_ref[:, cs], preferred_element_type=F32)
               + jnp.dot((pc / den).astype(BF16), vc_ref[:, cs], preferred_element_type=F32))
        o_ref[:, cs] = out
        l_ref[:, cs] = jnp.broadcast_to(m + jnp.log(den), (blk, HEAD_DIM))


def _attention_group(qkv, g, window, dilation):
    bsz, _, length, _ = qkv.shape
    blk = window // dilation
    nb = length // blk
    cur = lambda c: (lambda b, r, n: (b, r, n, c))
    prev = lambda c: (lambda b, r, n: (b, r, jnp.maximum(n - 1, 0), c))
    spec = lambda f: pl.BlockSpec((None, None, blk, GROUP_COLS), f)
    out_sds = jax.ShapeDtypeStruct((bsz, dilation, length, GROUP_COLS), F32)
    return pl.pallas_call(
        _attn_kernel,
        grid=(bsz, dilation, nb),
        in_specs=[spec(cur(0)), spec(cur(1)), spec(prev(1)), spec(cur(2)), spec(prev(2))],
        out_specs=[spec(cur(0)), spec(cur(0))],
        out_shape=[out_sds, out_sds],
        compiler_params=_params("arbitrary", "arbitrary", "arbitrary"),
        name=f"attn_g{g}",
    )(qkv, qkv, qkv, qkv, qkv)


def _merge_kernel(ya_ref, o0, o1, o2, l0, l1, l2, ga_ref, gb_ref, wr_ref, wa_ref, m_ref,
                  yb_scr, o_scr, l_scr):
    tm = ya_ref.shape[0]

    @pl.when(pl.program_id(1) == 0)
    def _():
        for g, (o_ref, l_ref) in enumerate(((o0, l0), (o1, l1), (o2, l2))):
            dil = o_ref.shape[0]
            for r in range(dil):
                for h in range(HEADS_PER_GROUP):
                    cs = slice(h * HEAD_DIM, (h + 1) * HEAD_DIM)
                    if dil == 1:
                        o_scr[g, h] = o_ref[r, :, cs]
                        l_scr[g, h] = l_ref[r, :, cs]
                    else:
                        o_scr[g, h, pl.ds(r, tm // dil, stride=dil), :] = o_ref[r, :, cs]
                        l_scr[g, h, pl.ds(r, tm // dil, stride=dil), :] = l_ref[r, :, cs]
        for h in range(HEADS_PER_GROUP):
            la, lb, lc = l_scr[0, h], l_scr[1, h], l_scr[2, h]
            m = jnp.maximum(jnp.maximum(la, lb), lc)
            ea, eb, ec = jnp.exp(la - m), jnp.exp(lb - m), jnp.exp(lc - m)
            tot = ea + eb + ec
            yb = (ea / tot) * o_scr[0, h] + (eb / tot) * o_scr[1, h] + (ec / tot) * o_scr[2, h]
            yb_scr[:, h * HEAD_DIM:(h + 1) * HEAD_DIM] = yb.astype(BF16)

    pa = jnp.dot(ya_ref[...], wr_ref[...], preferred_element_type=F32)
    pb = jnp.dot(yb_scr[...], wa_ref[...], preferred_element_type=F32)
    m_ref[...] = (ga_ref[...].astype(F32) * pa + gb_ref[...].astype(F32) * pb).astype(m_ref.dtype)


def _merge(ya, outs, lses, gates, wr_b, wa_b, seq):
    t, d_rnn = ya.shape
    d = wr_b.shape[1]
    tm, tn = 512, GROUP_COLS
    per_b = seq // tm
    n_groups = len(outs)

    def grp(o):
        dil = o.shape[1]
        return pl.BlockSpec((None, dil, tm // dil, GROUP_COLS), lambda i, j: (i // per_b, 0, i % per_b, 0))

    return pl.pallas_call(
        _merge_kernel,
        grid=(t // tm, d // tn),
        in_specs=[
            pl.BlockSpec((tm, d_rnn), lambda i, j: (i, 0)),
            *[grp(o) for o in outs], *[grp(l) for l in lses],
            pl.BlockSpec((tm, tn), lambda i, j: (i, j)),
            pl.BlockSpec((tm, tn), lambda i, j: (i, d // tn + j)),
            pl.BlockSpec((d_rnn, tn), lambda i, j: (0, j)),
            pl.BlockSpec((GROUP_COLS, tn), lambda i, j: (0, j)),
        ],
        out_specs=pl.BlockSpec((tm, tn), lambda i, j: (i, j)),
        out_shape=jax.ShapeDtypeStruct((t, d), BF16),
        scratch_shapes=[pltpu.VMEM((tm, GROUP_COLS), BF16),
                        pltpu.VMEM((n_groups, HEADS_PER_GROUP, tm, HEAD_DIM), F32),
                        pltpu.VMEM((n_groups, HEADS_PER_GROUP, tm, HEAD_DIM), F32)],
        compiler_params=_params("arbitrary", "arbitrary"),
        name="merge_proj",
    )(ya, *outs, *lses, gates, gates, wr_b, wa_b)


def _outproj_kernel(m_ref, w_ref, x_ref, gate_ref, g2_ref, sh_ref, sc_ref, x1_ref, h2_ref):
    x1 = x_ref[...] + gate_ref[...] * jnp.dot(m_ref[...], w_ref[...], preferred_element_type=F32)
    x1_ref[...] = x1
    h2_ref[...] = _rms(x1, g2_ref[...]) * (1.0 + sc_ref[...]) + sh_ref[...]


def _out_proj(merged, w_out_b, x2, gate1, g2, shift2, scale2, seq):
    t, d = x2.shape
    tm = 512
    per_b = seq // tm
    row = pl.BlockSpec((tm, d), lambda i: (i, 0))
    bvec = pl.BlockSpec((None, 1, d), lambda i: (i // per_b, 0, 0))
    return pl.pallas_call(
        _outproj_kernel,
        grid=(t // tm,),
        in_specs=[row, pl.BlockSpec((d, d), lambda i: (0, 0)), row, bvec,
                  pl.BlockSpec((1, d), lambda i: (0, 0)), bvec, bvec],
        out_specs=[row, row],
        out_shape=[jax.ShapeDtypeStruct((t, d), F32)] * 2,
        compiler_params=_params("arbitrary"),
        name="out_proj",
    )(merged, w_out_b, x2, gate1, g2, shift2, scale2)


def _router_kernel(h_ref, rw_ref, bias_ref, idx_ref, w_ref, rank_ref, cnt_ref, carry):
    ne = rw_ref.shape[0]
    tm = h_ref.shape[0]
    gsz = ne // N_EXPERT_GROUPS

    @pl.when(pl.program_id(0) == 0)
    def _():
        carry[...] = jnp.zeros_like(carry)

    logits = lax.dot_general(rw_ref[...], h_ref[...].astype(BF16), (((1,), (1,)), ((), ())),
                             preferred_element_type=F32)
    scores = jax.nn.sigmoid(logits)
    sel = scores + bias_ref[...]
    row = lax.broadcasted_iota(I32, (ne, tm), 0)
    neg = -jnp.inf

    gscore = []
    rg = lax.broadcasted_iota(I32, (gsz, tm), 0)
    for g in range(N_EXPERT_GROUPS):
        sg = sel[g * gsz:(g + 1) * gsz, :]
        m1 = jnp.max(sg, axis=0, keepdims=True)
        i1 = jnp.min(jnp.where(sg == m1, rg, ne), axis=0, keepdims=True)
        m2 = jnp.max(jnp.where(rg == i1, neg, sg), axis=0, keepdims=True)
        gscore.append(m1 + m2)
    keep_rows = []
    for g in range(N_EXPERT_GROUPS):
        beaten = jnp.zeros((1, tm), I32)
        for o in range(N_EXPERT_GROUPS):
            if o == g:
                continue
            wins = (gscore[o] >= gscore[g]) if o < g else (gscore[o] > gscore[g])
            beaten = beaten + wins.astype(I32)
        keep_rows.append(jnp.broadcast_to(beaten, (gsz, tm)))
    cur = jnp.where(jnp.concatenate(keep_rows, axis=0) < TOPK_GROUPS, sel, neg)

    chosen = jnp.zeros((ne, tm), F32)
    picks, wts = [], []
    for _ in range(TOP_K):
        m = jnp.max(cur, axis=0, keepdims=True)
        ik = jnp.min(jnp.where(cur == m, row, ne), axis=0, keepdims=True)
        hit = row == ik
        wts.append(jnp.sum(jnp.where(hit, scores, 0.0), axis=0, keepdims=True))
        cur = jnp.where(hit, neg, cur)
        chosen = jnp.where(hit, 1.0, chosen)
        picks.append(ik)
    wsum = wts[0]
    for k in range(1, TOP_K):
        wsum = wsum + wts[k]

    ti = lax.broadcasted_iota(I32, (tm, tm), 0)
    tj = lax.broadcasted_iota(I32, (tm, tm), 1)
    upper = (ti < tj).astype(BF16)
    chosen_b = chosen.astype(BF16)
    before = jnp.dot(chosen_b, upper, preferred_element_type=F32)
    total = jnp.dot(chosen_b, jnp.ones((tm, LANES), BF16), preferred_element_type=F32)
    base = carry[...]
    pos = before + jnp.concatenate([base] * (tm // LANES), axis=1)
    for k in range(TOP_K):
        hit = row == picks[k]
        idx_ref[k:k + 1, :] = picks[k]
        w_ref[k:k + 1, :] = wts[k] / wsum * ROUTED_SCALE
        rank_ref[k:k + 1, :] = jnp.sum(jnp.where(hit, pos, 0.0), axis=0, keepdims=True).astype(I32)
    carry[...] = base + total
    cnt_ref[...] = base + total


def _router(h2, rw_t, bias_col):
    t, d = h2.shape
    ne = rw_t.shape[0]
    tm = 256
    kt = pl.BlockSpec((TOP_K, tm), lambda i: (0, i))
    return pl.pallas_call(
        _router_kernel,
        grid=(t // tm,),
        in_specs=[pl.BlockSpec((tm, d), lambda i: (i, 0)),
                  pl.BlockSpec((ne, d), lambda i: (0, 0)),
                  pl.BlockSpec((ne, 1), lambda i: (0, 0))],
        out_specs=[kt, kt, kt, pl.BlockSpec((ne, LANES), lambda i: (0, 0))],
        out_shape=[jax.ShapeDtypeStruct((TOP_K, t), I32), jax.ShapeDtypeStruct((TOP_K, t), F32),
                   jax.ShapeDtypeStruct((TOP_K, t), I32), jax.ShapeDtypeStruct((ne, LANES), F32)],
        scratch_shapes=[pltpu.VMEM((ne, LANES), F32)],
        compiler_params=_params("arbitrary"),
        name="router",
    )(h2, rw_t, bias_col)


def _dest_kernel(start_ref, idx_ref, rank_ref, dest_ref):
    ne = start_ref.shape[0]
    idx = idx_ref[...]

    def body(e, acc):
        return jnp.where(idx == e, start_ref[e], acc)

    dest = rank_ref[...] + lax.fori_loop(0, ne, body, jnp.zeros(idx.shape, I32))
    for j in range(dest_ref.shape[0]):
        dest_ref[j] = dest[:, j * DEST_TOKENS:(j + 1) * DEST_TOKENS]


def _dest_rows(starts, idx_t, rank_t):
    t = idx_t.shape[1]
    tb = 2048
    per = tb // DEST_TOKENS
    return pl.pallas_call(
        _dest_kernel,
        grid_spec=pltpu.PrefetchScalarGridSpec(
            num_scalar_prefetch=1,
            grid=(t // tb,),
            in_specs=[pl.BlockSpec((TOP_K, tb), lambda i, s: (0, i)),
                      pl.BlockSpec((TOP_K, tb), lambda i, s: (0, i))],
            out_specs=pl.BlockSpec((per, TOP_K, DEST_TOKENS), lambda i, s: (i, 0, 0)),
        ),
        out_shape=jax.ShapeDtypeStruct((t // DEST_TOKENS, TOP_K, DEST_TOKENS), I32),
        compiler_params=_params("arbitrary"),
        name="dest_rows",
    )(starts, idx_t, rank_t)


def _dispatch_kernel(dest_hbm, h_ref, xs_hbm, idx_smem, isem, dsem):
    i = pl.program_id(0)
    n = pl.num_programs(0)
    tm = h_ref.shape[0]
    slot = i % 2

    def idx_copy(tile, s):
        return pltpu.make_async_copy(dest_hbm.at[tile], idx_smem.at[s], isem.at[s])

    @pl.when(i == 0)
    def _():
        idx_copy(0, 0).start()

    idx_copy(i, slot).wait()

    @pl.when(i + 1 < n)
    def _():
        idx_copy(i + 1, 1 - slot).start()

    def body(t, c):
        for k in range(TOP_K):
            d = idx_smem[slot, k, t]
            pltpu.make_async_copy(h_ref.at[pl.ds(t, 1), :], xs_hbm.at[pl.ds(d, 1), :], dsem).start()
        return c

    lax.fori_loop(0, tm, body, 0)
    for k in range(TOP_K):
        pltpu.make_async_copy(h_ref, xs_hbm.at[pl.ds(0, tm), :], dsem).wait()


def _dispatch(dest, h2, n_rows):
    t, d = h2.shape
    tm = DEST_TOKENS
    return pl.pallas_call(
        _dispatch_kernel,
        grid=(t // tm,),
        in_specs=[pl.BlockSpec(memory_space=pl.ANY), pl.BlockSpec((tm, d), lambda i: (i, 0))],
        out_specs=pl.BlockSpec(memory_space=pl.ANY),
        out_shape=jax.ShapeDtypeStruct((n_rows, d), F32),
        scratch_shapes=[pltpu.SMEM((2, TOP_K, tm), I32), pltpu.SemaphoreType.DMA((2,)),
                        pltpu.SemaphoreType.DMA(())],
        compiler_params=_params("arbitrary"),
        name="dispatch",
    )(dest, h2)


def _experts_kernel(e_ref, b_ref, lo_ref, hi_ref, n_ref, xs_ref, w1_ref, w3_ref, w2_ref, ys_ref, w1b, w3b, w2b):
    w = pl.program_id(0)
    prev = jnp.maximum(w - 1, 0)

    @pl.when(w < n_ref[0])
    def _():
        @pl.when(jnp.logical_or(w == 0, e_ref[w] != e_ref[prev]))
        def _():
            w1b[...] = w1_ref[...].astype(BF16)
            w3b[...] = w3_ref[...].astype(BF16)
            w2b[...] = w2_ref[...].astype(BF16)

        x = xs_ref[...].astype(BF16)
        h1 = jnp.dot(x, w1b[...], preferred_element_type=F32)
        h3 = jnp.dot(x, w3b[...], preferred_element_type=F32)
        act = (_silu(h1) * h3).astype(BF16)
        y = jnp.dot(act, w2b[...], preferred_element_type=F32)
        row = b_ref[w] * EXPERT_ROWS + lax.broadcasted_iota(I32, (EXPERT_ROWS, 1), 0)
        mine = jnp.logical_and(row >= lo_ref[w], row < hi_ref[w])
        new_block = jnp.logical_or(w == 0, b_ref[w] != b_ref[prev])

        @pl.when(new_block)
        def _():
            ys_ref[...] = jnp.where(mine, y, 0.0)

        @pl.when(jnp.logical_not(new_block))
        def _():
            ys_ref[...] = jnp.where(mine, y, ys_ref[...])


def _experts(item_expert, item_block, item_lo, item_hi, n_items, xs, w1, w3, w2):
    n_rows, d = xs.shape
    de = w1.shape[2]
    rows = lambda w, e, b, lo, hi, n: (b[w], 0)
    wsel = lambda w, e, b, lo, hi, n: (e[w], 0, 0)
    return pl.pallas_call(
        _experts_kernel,
        grid_spec=pltpu.PrefetchScalarGridSpec(
            num_scalar_prefetch=5,
            grid=(item_expert.shape[0],),
            in_specs=[pl.BlockSpec((EXPERT_ROWS, d), rows),
                      pl.BlockSpec((None, d, de), wsel),
                      pl.BlockSpec((None, d, de), wsel),
                      pl.BlockSpec((None, de, d), wsel)],
            out_specs=pl.BlockSpec((EXPERT_ROWS, d), rows),
            scratch_shapes=[pltpu.VMEM((d, de), BF16), pltpu.VMEM((d, de), BF16), pltpu.VMEM((de, d), BF16)],
        ),
        out_shape=jax.ShapeDtypeStruct((n_rows, d), F32),
        compiler_params=_params("arbitrary"),
        name="experts",
    )(item_expert, item_block, item_lo, item_hi, n_items, xs, w1, w3, w2)


def _shared_kernel(h_ref, w1_ref, w3_ref, w2_ref, y_ref):
    x = h_ref[...].astype(BF16)
    h1 = jnp.dot(x, w1_ref[...], preferred_element_type=F32)
    h3 = jnp.dot(x, w3_ref[...], preferred_element_type=F32)
    y_ref[...] = jnp.dot((_silu(h1) * h3).astype(BF16), w2_ref[...], preferred_element_type=F32)


def _shared(h2, w1_b, w3_b, w2_b):
    t, d = h2.shape
    de = w1_b.shape[1]
    tm = 512
    row = pl.BlockSpec((tm, d), lambda i: (i, 0))
    return pl.pallas_call(
        _shared_kernel,
        grid=(t // tm,),
        in_specs=[row, pl.BlockSpec((d, de), lambda i: (0, 0)), pl.BlockSpec((d, de), lambda i: (0, 0)),
                  pl.BlockSpec((de, d), lambda i: (0, 0))],
        out_specs=row,
        out_shape=jax.ShapeDtypeStruct((t, d), F32),
        compiler_params=_params("arbitrary"),
        name="shared_expert",
    )(h2, w1_b, w3_b, w2_b)


def _combine_kernel(dest_hbm, ys_hbm, x1_ref, ysh_ref, wt_ref, gate_ref, fg_ref, o_ref,
                    rows, idx_smem, isem, gsem):
    i = pl.program_id(0)
    n = pl.num_programs(0)
    tm = x1_ref.shape[0]
    slot = i % 2

    def idx_copy(tile, s):
        return pltpu.make_async_copy(dest_hbm.at[tile], idx_smem.at[s], isem.at[s])

    def issue_gathers(s):
        def body(t, c):
            for k in range(TOP_K):
                d = idx_smem[s, k, t]
                pltpu.make_async_copy(ys_hbm.at[pl.ds(d, 1), :], rows.at[s, k, pl.ds(t, 1), :],
                                      gsem.at[s]).start()
            return c

        lax.fori_loop(0, tm, body, 0)

    @pl.when(i == 0)
    def _():
        idx_copy(0, 0).start()
        idx_copy(0, 0).wait()
        issue_gathers(0)

        @pl.when(n > 1)
        def _():
            idx_copy(1, 1).start()

    @pl.when(i + 1 < n)
    def _():
        idx_copy(i + 1, 1 - slot).wait()
        issue_gathers(1 - slot)

    @pl.when(i + 2 < n)
    def _():
        idx_copy(i + 2, slot).start()

    for k in range(TOP_K):
        pltpu.make_async_copy(ys_hbm.at[pl.ds(0, tm), :], rows.at[slot, k], gsem.at[slot]).wait()

    w = wt_ref[...]
    acc = ysh_ref[...]
    for k in range(TOP_K):
        acc = acc + rows[slot, k] * w[:, k:k + 1]
    o_ref[...] = _rms(x1_ref[...] + gate_ref[...] * acc, fg_ref[...])


def _combine(dest, ys, x1, ysh, w_tok, gate2, final_g, seq):
    t, d = x1.shape
    tm = DEST_TOKENS
    per_b = seq // tm
    row = pl.BlockSpec((tm, d), lambda i: (i, 0))
    return pl.pallas_call(
        _combine_kernel,
        grid=(t // tm,),
        in_specs=[pl.BlockSpec(memory_space=pl.ANY), pl.BlockSpec(memory_space=pl.ANY), row, row,
                  pl.BlockSpec((tm, TOP_K), lambda i: (i, 0)),
                  pl.BlockSpec((None, 1, d), lambda i: (i // per_b, 0, 0)),
                  pl.BlockSpec((1, d), lambda i: (0, 0))],
        out_specs=row,
        out_shape=jax.ShapeDtypeStruct((t, d), F32),
        scratch_shapes=[pltpu.VMEM((2, TOP_K, tm, d), F32), pltpu.SMEM((2, TOP_K, tm), I32),
                        pltpu.SemaphoreType.DMA((2,)), pltpu.SemaphoreType.DMA((2,))],
        compiler_params=_params("arbitrary"),
        name="combine",
    )(dest, ys, x1, ysh, w_tok, gate2, final_g)


def _mixer(x2, mod6, cos_t, sin_t, bsz, seq, p):
    t, d = x2.shape
    shift1, scale1, gate1, shift2, scale2, _ = mod6
    d_rnn = p["conv_w"].shape[1]
    att_width = len(DILATION_GROUPS) * GROUP_COLS
    q_col = 2 * d_rnn
    gate_col = 2 * d_rnn + 3 * att_width
    w_in_b = p["w_in"].astype(BF16)

    h1 = _norm_mod(x2, p["norm1_g"].reshape(1, d), shift1, scale1, seq)
    rnn = _proj_act(h1, w_in_b, 0, 2 * d_rnn, d_rnn // GROUP_COLS, _gelu_tanh, "proj_rnn")
    gates = _proj_act(h1, w_in_b, gate_col, 2 * d, 0, jax.nn.sigmoid, "proj_gates")
    ya = _rglru(rnn, p["conv_w"], p["conv_b"].reshape(1, d_rnn),
                p["rg_wa"].astype(BF16), p["rg_ba"].reshape(1, d_rnn),
                p["rg_wi"].astype(BF16), p["rg_bi"].reshape(1, d_rnn),
                p["rg_lambda"].reshape(1, d_rnn), bsz, seq, d_rnn)

    outs, lses = [], []
    for g, (window, dilation) in enumerate(DILATION_GROUPS):
        qkv = _proj_qkv(h1, w_in_b, cos_t, sin_t, g, dilation, bsz, seq, q_col, att_width)
        o, l = _attention_group(qkv, g, window, dilation)
        outs.append(o)
        lses.append(l)

    merged = _merge(ya, outs, lses, gates, p["w_proj_rnn"].astype(BF16), p["w_proj_attn"].astype(BF16), seq)
    return _out_proj(merged, p["w_out"].astype(BF16), x2, gate1, p["norm2_g"].reshape(1, d),
                     shift2, scale2, seq)


def _moe(h2, p):
    t, d = h2.shape
    ne = p["router_w"].shape[1]
    idx_t, w_t, rank_t, cnt = _router(h2, p["router_w"].T.astype(BF16), p["router_bias"].reshape(ne, 1))

    counts = cnt[:, 0].astype(I32)
    ends = jnp.cumsum(counts).astype(I32)
    starts = ends - counts
    n_rows = t * TOP_K
    first_blk = starts // EXPERT_ROWS
    n_blk_e = jnp.where(counts > 0, (ends - 1) // EXPERT_ROWS - first_blk + 1, 0)
    item_end = jnp.cumsum(n_blk_e).astype(I32)
    item_start = item_end - n_blk_e
    n_items = item_end[-1]
    max_items = n_rows // EXPERT_ROWS + ne
    w = jnp.minimum(jnp.arange(max_items, dtype=I32), n_items - 1)
    item_expert = jnp.minimum(jnp.searchsorted(item_end, w, side="right"), ne - 1).astype(I32)
    item_block = first_blk[item_expert] + (w - item_start[item_expert])

    dest = _dest_rows(starts, idx_t, rank_t)
    xs = _dispatch(dest, h2, n_rows)
    ys = _experts(item_expert, item_block, starts[item_expert], ends[item_expert], n_items.reshape(1),
                  xs, p["exp_w1"], p["exp_w3"], p["exp_w2"])
    ysh = _shared(h2, p["sh_w1"].astype(BF16), p["sh_w3"].astype(BF16), p["sh_w2"].astype(BF16))
    return dest, ys, ysh, w_t.T


def kernel(x, c, positions, ada_w, ada_b, norm1_g, w_in, conv_w, conv_b, rg_wa, rg_ba, rg_wi, rg_bi, rg_lambda, w_proj_rnn, w_proj_attn, w_out, norm2_g, router_w, router_bias, exp_w1, exp_w3, exp_w2, sh_w1, sh_w3, sh_w2, final_g):
    bsz, seq, d = x.shape
    assert ada_w.shape[0] == 1, "the fused final norm assumes a single layer"
    t = bsz * seq
    x2 = x.reshape(t, d)
    first = lambda a: a.reshape(a.shape[1:])

    half = HEAD_DIM // 2
    inv_freq = ROPE_THETA ** (-jnp.arange(half, dtype=F32) * 2.0 / HEAD_DIM)
    freq = jnp.concatenate([inv_freq, inv_freq]).reshape(1, HEAD_DIM)
    sign = jnp.concatenate([-jnp.ones((half,), F32), jnp.ones((half,), F32)]).reshape(1, HEAD_DIM)
    cos_t, sin_t = _rope_tables(positions.reshape(t, 1), freq, sign)

    c_pad = jnp.zeros((SUBLANES, d), F32).at[:bsz].set(c)
    mod = _ada_mod(c_pad, first(ada_w), ada_b.reshape(1, -1))
    mod6 = tuple(mod[:bsz, k * d:(k + 1) * d].reshape(bsz, 1, d) for k in range(6))

    p = dict(norm1_g=first(norm1_g), w_in=first(w_in), conv_w=first(conv_w), conv_b=first(conv_b),
             rg_wa=first(rg_wa), rg_ba=first(rg_ba), rg_wi=first(rg_wi), rg_bi=first(rg_bi),
             rg_lambda=first(rg_lambda), w_proj_rnn=first(w_proj_rnn), w_proj_attn=first(w_proj_attn),
             w_out=first(w_out), norm2_g=first(norm2_g), router_w=first(router_w),
             router_bias=first(router_bias), exp_w1=first(exp_w1), exp_w3=first(exp_w3),
             exp_w2=first(exp_w2), sh_w1=first(sh_w1), sh_w3=first(sh_w3), sh_w2=first(sh_w2))
    x1, h2 = _mixer(x2, mod6, cos_t, sin_t, bsz, seq, p)
    dest, ys, ysh, w_tok = _moe(h2, p)
    out = _combine(dest, ys, x1, ysh, w_tok, mod6[5], final_g.reshape(1, d), seq)
    return out.reshape(bsz, seq, d)
```

```python
import functools
import math

import jax
import jax.numpy as jnp
from jax import lax
from jax.experimental import pallas as pl
from jax.experimental.pallas import tpu as pltpu

F32 = jnp.float32
BF16 = jnp.bfloat16
I32 = jnp.int32

HEAD_DIM = 128
HEADS_PER_GROUP = 4
DILATION_GROUPS = ((128, 1), (512, 4), (2048, 16))
ROPE_THETA = 10000.0
CONV_WIDTH = 4
LRU_C = 8.0
TOP_K = 8
N_EXPERT_GROUPS = 8
TOPK_GROUPS = 4
ROUTED_SCALE = 2.5
NORM_EPS = 1e-6

LANES = 128
SUBLANES = 8
VMEM_LIMIT_BYTES = 56 * 1024 * 1024

GROUP_COLS = HEADS_PER_GROUP * HEAD_DIM
EXPERT_ROWS = 256
DEST_TOKENS = 128


def _params(*sem):
    return pltpu.CompilerParams(dimension_semantics=sem, vmem_limit_bytes=VMEM_LIMIT_BYTES)


def _gelu_tanh(x):
    return 0.5 * x * (1.0 + jnp.tanh(math.sqrt(2.0 / math.pi) * (x + 0.044715 * (x * x * x))))


def _silu(x):
    return x * jax.nn.sigmoid(x)


def _rms(x, g):
    ms = jnp.mean(x * x, axis=-1, keepdims=True)
    return x * lax.rsqrt(ms + NORM_EPS) * g


def _ada_kernel(c_ref, w_ref, b_ref, o_ref):
    a = _silu(c_ref[...]).astype(BF16)
    o_ref[...] = jnp.dot(a, w_ref[...].astype(BF16), preferred_element_type=F32) + b_ref[...]


def _ada_mod(c_pad, ada_w, ada_b):
    rows, d = c_pad.shape
    n = ada_w.shape[1]
    tn = 1024
    return pl.pallas_call(
        _ada_kernel,
        grid=(n // tn,),
        in_specs=[
            pl.BlockSpec((rows, d), lambda j: (0, 0)),
            pl.BlockSpec((d, tn), lambda j: (0, j)),
            pl.BlockSpec((1, tn), lambda j: (0, j)),
        ],
        out_specs=pl.BlockSpec((rows, tn), lambda j: (0, j)),
        out_shape=jax.ShapeDtypeStruct((rows, n), F32),
        compiler_params=_params("arbitrary"),
        name="ada_mod",
    )(c_pad, ada_w, ada_b)


def _rope_kernel(pos_ref, freq_ref, sign_ref, cos_ref, sin_ref):
    ang = pos_ref[...].astype(F32) * freq_ref[...]
    cos_ref[...] = jnp.cos(ang)
    sin_ref[...] = jnp.sin(ang) * sign_ref[...]


def _rope_tables(pos_col, freq, sign):
    t = pos_col.shape[0]
    tm = 1024
    return pl.pallas_call(
        _rope_kernel,
        grid=(t // tm,),
        in_specs=[
            pl.BlockSpec((tm, 1), lambda i: (i, 0)),
            pl.BlockSpec((1, HEAD_DIM), lambda i: (0, 0)),
            pl.BlockSpec((1, HEAD_DIM), lambda i: (0, 0)),
        ],
        out_specs=[pl.BlockSpec((tm, HEAD_DIM), lambda i: (i, 0))] * 2,
        out_shape=[jax.ShapeDtypeStruct((t, HEAD_DIM), F32)] * 2,
        compiler_params=_params("arbitrary"),
        name="rope_tables",
    )(pos_col, freq, sign)


def _norm_kernel(x_ref, g_ref, sh_ref, sc_ref, h_ref):
    h_ref[...] = (_rms(x_ref[...], g_ref[...]) * (1.0 + sc_ref[...]) + sh_ref[...]).astype(h_ref.dtype)


def _norm_mod(x2, g, shift, scale, seq):
    t, d = x2.shape
    tm = 1024
    per_b = seq // tm
    bvec = pl.BlockSpec((None, 1, d), lambda i: (i // per_b, 0, 0))
    return pl.pallas_call(
        _norm_kernel,
        grid=(t // tm,),
        in_specs=[pl.BlockSpec((tm, d), lambda i: (i, 0)), pl.BlockSpec((1, d), lambda i: (0, 0)), bvec, bvec],
        out_specs=pl.BlockSpec((tm, d), lambda i: (i, 0)),
        out_shape=jax.ShapeDtypeStruct((t, d), BF16),
        compiler_params=_params("arbitrary"),
        name="norm_mod",
    )(x2, g, shift, scale)


def _proj_act_kernel(h_ref, w_ref, o_ref, *, plain_tiles, act):
    acc = jnp.dot(h_ref[...], w_ref[...], preferred_element_type=F32)
    j = pl.program_id(1)

    @pl.when(j < plain_tiles)
    def _():
        o_ref[...] = acc.astype(o_ref.dtype)

    @pl.when(j >= plain_tiles)
    def _():
        o_ref[...] = act(acc).astype(o_ref.dtype)


def _proj_act(h1, w_in_b, col0, ncols, plain_tiles, act, name):
    t, d = h1.shape
    tm, tn = 1024, GROUP_COLS
    j0 = col0 // tn
    return pl.pallas_call(
        functools.partial(_proj_act_kernel, plain_tiles=plain_tiles, act=act),
        grid=(t // tm, ncols // tn),
        in_specs=[pl.BlockSpec((tm, d), lambda i, j: (i, 0)),
                  pl.BlockSpec((d, tn), lambda i, j: (0, j0 + j))],
        out_specs=pl.BlockSpec((tm, tn), lambda i, j: (i, j)),
        out_shape=jax.ShapeDtypeStruct((t, ncols), BF16),
        compiler_params=_params("arbitrary", "arbitrary"),
        name=name,
    )(h1, w_in_b)


def _proj_qkv_kernel(h_ref, w_ref, cos_ref, sin_ref, o_ref, scr, *, dilation):
    tm = h_ref.shape[0]
    acc = jnp.dot(h_ref[...], w_ref[...], preferred_element_type=F32)
    j = pl.program_id(1)

    @pl.when(j < 2)
    def _():
        scale = jnp.where(j == 0, HEAD_DIM ** -0.5, 1.0).astype(F32)
        c = cos_ref[...] * scale
        s = sin_ref[...] * scale
        for h in range(HEADS_PER_GROUP):
            v = acc[:, h * HEAD_DIM:(h + 1) * HEAD_DIM]
            scr[h] = v * c + pltpu.roll(v, HEAD_DIM // 2, axis=1) * s

    @pl.when(j == 2)
    def _():
        for h in range(HEADS_PER_GROUP):
            scr[h] = acc[:, h * HEAD_DIM:(h + 1) * HEAD_DIM]

    sub = tm // dilation
    for r in range(dilation):
        for h in range(HEADS_PER_GROUP):
            rows = scr[h] if dilation == 1 else scr[h, pl.ds(r, sub, stride=dilation), :]
            o_ref[r, :, h * HEAD_DIM:(h + 1) * HEAD_DIM] = rows.astype(o_ref.dtype)


def _proj_qkv(h1, w_in_b, cos_t, sin_t, g, dilation, bsz, seq, q_col, att_width):
    t, d = h1.shape
    tm, tn = 1024, GROUP_COLS
    per_b = seq // tm
    j0 = q_col // tn + g
    step = att_width // tn
    sub = tm // dilation
    return pl.pallas_call(
        functools.partial(_proj_qkv_kernel, dilation=dilation),
        grid=(t // tm, 3),
        in_specs=[pl.BlockSpec((tm, d), lambda i, j: (i, 0)),
                  pl.BlockSpec((d, tn), lambda i, j: (0, j0 + step * j)),
                  pl.BlockSpec((tm, HEAD_DIM), lambda i, j: (i, 0)),
                  pl.BlockSpec((tm, HEAD_DIM), lambda i, j: (i, 0))],
        out_specs=pl.BlockSpec((None, dilation, sub, tn), lambda i, j: (i // per_b, 0, i % per_b, j)),
        out_shape=jax.ShapeDtypeStruct((bsz, dilation, seq // dilation, 3 * tn), BF16),
        scratch_shapes=[pltpu.VMEM((HEADS_PER_GROUP, tm, HEAD_DIM), F32)],
        compiler_params=_params("arbitrary", "arbitrary"),
        name=f"proj_qkv_g{g}",
    )(h1, w_in_b, cos_t, sin_t)


def _rglru_kernel(xr_ref, gr_ref, cw_ref, cb_ref, wa_ref, ba_ref, wi_ref, bi_ref, lam_ref,
                  ya_ref, xbuf, a_scr, b_scr, hcar):
    tt = xr_ref.shape[0]
    halo = SUBLANES

    @pl.when(pl.program_id(2) == 0)
    def _():
        xbuf[0:halo, :] = jnp.zeros((halo, LANES), F32)
        hcar[...] = jnp.zeros_like(hcar)

    xbuf[halo:halo + tt, :] = xr_ref[...].astype(F32)
    u = cb_ref[...] + cw_ref[CONV_WIDTH - 1:CONV_WIDTH, :] * xbuf[halo:halo + tt, :]
    for j in range(CONV_WIDTH - 1):
        s = CONV_WIDTH - 1 - j
        u = u + cw_ref[j:j + 1, :] * xbuf[halo - s:halo - s + tt, :]
    xbuf[0:halo, :] = xbuf[tt:tt + halo, :]

    ub = u.astype(BF16)
    r = 0.5 + 0.5 * jnp.tanh(0.5 * (jnp.dot(ub, wa_ref[...], preferred_element_type=F32) + ba_ref[...]))
    ig = 0.5 + 0.5 * jnp.tanh(0.5 * (jnp.dot(ub, wi_ref[...], preferred_element_type=F32) + bi_ref[...]))
    z = -lam_ref[...]
    softplus = jnp.maximum(z, 0.0) + jnp.log1p(jnp.exp(-jnp.abs(z)))
    t = jnp.tanh((-0.5 * LRU_C) * r * softplus)
    q = 1.0 / (1.0 - t)
    a_scr[...] = (1.0 + t) * q
    b_scr[...] = (2.0 * q) * jnp.sqrt(-t) * (ig * u)

    seg = tt // SUBLANES
    h = jnp.zeros((SUBLANES, LANES), F32)
    prod = jnp.ones((SUBLANES, LANES), F32)
    for j in range(seg):
        rows_j = pl.ds(j, SUBLANES, stride=seg)
        aj = a_scr[rows_j, :]
        h = aj * h + b_scr[rows_j, :]
        prod = aj * prod
        b_scr[rows_j, :] = h
        a_scr[rows_j, :] = prod
    c = hcar[0:1, :]
    entering = []
    for s in range(SUBLANES):
        entering.append(c)
        c = h[s:s + 1, :] + prod[s:s + 1, :] * c
    hcar[...] = jnp.broadcast_to(c, hcar.shape)
    start = jnp.concatenate(entering, axis=0)
    for j in range(seg):
        rows_j = pl.ds(j, SUBLANES, stride=seg)
        b_scr[rows_j, :] = b_scr[rows_j, :] + a_scr[rows_j, :] * start
    ya_ref[...] = (b_scr[...] * gr_ref[...].astype(F32)).astype(ya_ref.dtype)


def _rglru(rnn, conv_w, conv_b, wa_b, ba, wi_b, bi, lam, bsz, seq, d_rnn):
    t = rnn.shape[0]
    tt = 512
    nct = d_rnn // LANES
    per_b = seq // tt
    row = lambda b, c, s: (b * per_b + s, c)
    vec = lambda b, c, s: (0, c)
    return pl.pallas_call(
        _rglru_kernel,
        grid=(bsz, nct, per_b),
        in_specs=[
            pl.BlockSpec((tt, LANES), row),
            pl.BlockSpec((tt, LANES), lambda b, c, s: (b * per_b + s, nct + c)),
            pl.BlockSpec((CONV_WIDTH, LANES), vec),
            pl.BlockSpec((1, LANES), vec),
            pl.BlockSpec((None, LANES, LANES), lambda b, c, s: (c, 0, 0)),
            pl.BlockSpec((1, LANES), vec),
            pl.BlockSpec((None, LANES, LANES), lambda b, c, s: (c, 0, 0)),
            pl.BlockSpec((1, LANES), vec),
            pl.BlockSpec((1, LANES), vec),
        ],
        out_specs=pl.BlockSpec((tt, LANES), row),
        out_shape=jax.ShapeDtypeStruct((t, d_rnn), BF16),
        scratch_shapes=[pltpu.VMEM((tt + SUBLANES, LANES), F32), pltpu.VMEM((tt, LANES), F32),
                        pltpu.VMEM((tt, LANES), F32), pltpu.VMEM((SUBLANES, LANES), F32)],
        compiler_params=_params("arbitrary", "arbitrary", "arbitrary"),
        name="rglru",
    )(rnn, rnn, conv_w, conv_b, wa_b, ba, wi_b, bi, lam)


ATTN_BLOCKS = 4


def _attn_kernel(q_ref, kc_ref, kp_ref, vc_ref, vp_ref, o_ref, l_ref, *, blk):
    nq = q_ref.shape[0] // blk
    not_first = pl.program_id(2) > 0
    qi = lax.broadcasted_iota(I32, (blk, blk), 0)
    kj = lax.broadcasted_iota(I32, (blk, blk), 1)
    tri_prev = kj >= qi
    mask_cur = kj <= qi
    nt = (((1,), (1,)), ((), ()))
    for j in range(nq):
        rs = slice(j * blk, (j + 1) * blk)
        ps = slice((j - 1) * blk, j * blk)
        for h in range(HEADS_PER_GROUP):
            cs = slice(h * HEAD_DIM, (h + 1) * HEAD_DIM)
            q = q_ref[rs, cs]
            if j == 0:
                kp, vp, mask_prev = kp_ref[:, cs], vp_ref[:, cs], jnp.logical_and(tri_prev, not_first)
            else:
                kp, vp, mask_prev = kc_ref[ps, cs], vc_ref[ps, cs], tri_prev
            sp = lax.dot_general(q, kp, nt, preferred_element_type=F32)
            sc = lax.dot_general(q, kc_ref[rs, cs], nt, preferred_element_type=F32)
            sp = jnp.where(mask_prev, sp, -jnp.inf)
            sc = jnp.where(mask_cur, sc, -jnp.inf)
            m = jnp.maximum(jnp.max(sp, axis=-1, keepdims=True), jnp.max(sc, axis=-1, keepdims=True))
            pp = jnp.exp(sp - m)
            pc = jnp.exp(sc - m)
            den = jnp.sum(pp, axis=-1, keepdims=True) + jnp.sum(pc, axis=-1, keepdims=True)
            out = (jnp.dot((pp / den).astype(BF16), vp, preferred_element_type=F32)
                   + jnp.dot((pc / den).astype(BF16), vc_ref[rs, cs], preferred_element_type=F32))
            o_ref[rs, cs] = out
            l_ref[rs, cs] = jnp.broadcast_to(m + jnp.log(den), (blk, HEAD_DIM))


def _attention_group(qkv, g, window, dilation):
    bsz, _, length, _ = qkv.shape
    blk = window // dilation
    nb = length // blk
    nq = math.gcd(ATTN_BLOCKS, nb)
    cur = lambda c: (lambda b, r, n: (b, r, n, c))
    prev = lambda c: (lambda b, r, n: (b, r, jnp.maximum(n * nq - 1, 0), c))
    spec = lambda f: pl.BlockSpec((None, None, nq * blk, GROUP_COLS), f)
    pspec = lambda f: pl.BlockSpec((None, None, blk, GROUP_COLS), f)
    out_sds = jax.ShapeDtypeStruct((bsz, dilation, length, GROUP_COLS), F32)
    return pl.pallas_call(
        functools.partial(_attn_kernel, blk=blk),
        grid=(bsz, dilation, nb // nq),
        in_specs=[spec(cur(0)), spec(cur(1)), pspec(prev(1)), spec(cur(2)), pspec(prev(2))],
        out_specs=[spec(cur(0)), spec(cur(0))],
        out_shape=[out_sds, out_sds],
        compiler_params=_params("arbitrary", "arbitrary", "arbitrary"),
        name=f"attn_g{g}",
    )(qkv, qkv, qkv, qkv, qkv)


def _merge_kernel(ya_ref, o0, o1, o2, l0, l1, l2, ga_ref, gb_ref, wr_ref, wa_ref, m_ref,
                  yb_scr, o_scr, l_scr):
    tm = ya_ref.shape[0]

    @pl.when(pl.program_id(1) == 0)
    def _():
        for g, (o_ref, l_ref) in enumerate(((o0, l0), (o1, l1), (o2, l2))):
            dil = o_ref.shape[0]
            for r in range(dil):
                for h in range(HEADS_PER_GROUP):
                    cs = slice(h * HEAD_DIM, (h + 1) * HEAD_DIM)
                    if dil == 1:
                        o_scr[g, h] = o_ref[r, :, cs]
                        l_scr[g, h] = l_ref[r, :, cs]
                    else:
                        o_scr[g, h, pl.ds(r, tm // dil, stride=dil), :] = o_ref[r, :, cs]
                        l_scr[g, h, pl.ds(r, tm // dil, stride=dil), :] = l_ref[r, :, cs]
        for h in range(HEADS_PER_GROUP):
            la, lb, lc = l_scr[0, h], l_scr[1, h], l_scr[2, h]
            m = jnp.maximum(jnp.maximum(la, lb), lc)
            ea, eb, ec = jnp.exp(la - m), jnp.exp(lb - m), jnp.exp(lc - m)
            tot = ea + eb + ec
            yb = (ea / tot) * o_scr[0, h] + (eb / tot) * o_scr[1, h] + (ec / tot) * o_scr[2, h]
            yb_scr[:, h * HEAD_DIM:(h + 1) * HEAD_DIM] = yb.astype(BF16)

    pa = jnp.dot(ya_ref[...], wr_ref[...], preferred_element_type=F32)
    pb = jnp.dot(yb_scr[...], wa_ref[...], preferred_element_type=F32)
    m_ref[...] = (ga_ref[...].astype(F32) * pa + gb_ref[...].astype(F32) * pb).astype(m_ref.dtype)


def _merge(ya, outs, lses, gates, wr_b, wa_b, seq):
    t, d_rnn = ya.shape
    d = wr_b.shape[1]
    tm, tn = 512, GROUP_COLS
    per_b = seq // tm
    n_groups = len(outs)

    def grp(o):
        dil = o.shape[1]
        return pl.BlockSpec((None, dil, tm // dil, GROUP_COLS), lambda i, j: (i // per_b, 0, i % per_b, 0))

    return pl.pallas_call(
        _merge_kernel,
        grid=(t // tm, d // tn),
        in_specs=[
            pl.BlockSpec((tm, d_rnn), lambda i, j: (i, 0)),
            *[grp(o) for o in outs], *[grp(l) for l in lses],
            pl.BlockSpec((tm, tn), lambda i, j: (i, j)),
            pl.BlockSpec((tm, tn), lambda i, j: (i, d // tn + j)),
            pl.BlockSpec((d_rnn, tn), lambda i, j: (0, j)),
            pl.BlockSpec((GROUP_COLS, tn), lambda i, j: (0, j)),
        ],
        out_specs=pl.BlockSpec((tm, tn), lambda i, j: (i, j)),
        out_shape=jax.ShapeDtypeStruct((t, d), BF16),
        scratch_shapes=[pltpu.VMEM((tm, GROUP_COLS), BF16),
                        pltpu.VMEM((n_groups, HEADS_PER_GROUP, tm, HEAD_DIM), F32),
                        pltpu.VMEM((n_groups, HEADS_PER_GROUP, tm, HEAD_DIM), F32)],
        compiler_params=_params("arbitrary", "arbitrary"),
        name="merge_proj",
    )(ya, *outs, *lses, gates, gates, wr_b, wa_b)


def _outproj_kernel(m_ref, w_ref, x_ref, gate_ref, g2_ref, sh_ref, sc_ref, x1_ref, h2_ref):
    x1 = x_ref[...] + gate_ref[...] * jnp.dot(m_ref[...], w_ref[...], preferred_element_type=F32)
    x1_ref[...] = x1
    h2_ref[...] = _rms(x1, g2_ref[...]) * (1.0 + sc_ref[...]) + sh_ref[...]


def _out_proj(merged, w_out_b, x2, gate1, g2, shift2, scale2, seq):
    t, d = x2.shape
    tm = 512
    per_b = seq // tm
    row = pl.BlockSpec((tm, d), lambda i: (i, 0))
    bvec = pl.BlockSpec((None, 1, d), lambda i: (i // per_b, 0, 0))
    return pl.pallas_call(
        _outproj_kernel,
        grid=(t // tm,),
        in_specs=[row, pl.BlockSpec((d, d), lambda i: (0, 0)), row, bvec,
                  pl.BlockSpec((1, d), lambda i: (0, 0)), bvec, bvec],
        out_specs=[row, row],
        out_shape=[jax.ShapeDtypeStruct((t, d), F32)] * 2,
        compiler_params=_params("arbitrary"),
        name="out_proj",
    )(merged, w_out_b, x2, gate1, g2, shift2, scale2)


def _router_kernel(h_ref, rw_ref, bias_ref, idx_ref, w_ref, rank_ref, cnt_ref, carry):
    ne = rw_ref.shape[0]
    tm = h_ref.shape[0]
    gsz = ne // N_EXPERT_GROUPS

    @pl.when(pl.program_id(0) == 0)
    def _():
        carry[...] = jnp.zeros_like(carry)

    logits = lax.dot_general(rw_ref[...], h_ref[...].astype(BF16), (((1,), (1,)), ((), ())),
                             preferred_element_type=F32)
    scores = jax.nn.sigmoid(logits)
    sel = scores + bias_ref[...]
    row = lax.broadcasted_iota(I32, (ne, tm), 0)
    neg = -jnp.inf

    gscore = []
    rg = lax.broadcasted_iota(I32, (gsz, tm), 0)
    for g in range(N_EXPERT_GROUPS):
        sg = sel[g * gsz:(g + 1) * gsz, :]
        m1 = jnp.max(sg, axis=0, keepdims=True)
        i1 = jnp.min(jnp.where(sg == m1, rg, ne), axis=0, keepdims=True)
        m2 = jnp.max(jnp.where(rg == i1, neg, sg), axis=0, keepdims=True)
        gscore.append(m1 + m2)
    keep_rows = []
    for g in range(N_EXPERT_GROUPS):
        beaten = jnp.zeros((1, tm), I32)
        for o in range(N_EXPERT_GROUPS):
            if o == g:
                continue
            wins = (gscore[o] >= gscore[g]) if o < g else (gscore[o] > gscore[g])
            beaten = beaten + wins.astype(I32)
        keep_rows.append(jnp.broadcast_to(beaten, (gsz, tm)))
    cur = jnp.where(jnp.concatenate(keep_rows, axis=0) < TOPK_GROUPS, sel, neg)

    chosen = jnp.zeros((ne, tm), F32)
    picks, wts = [], []
    for _ in range(TOP_K):
        m = jnp.max(cur, axis=0, keepdims=True)
        ik = jnp.min(jnp.where(cur == m, row, ne), axis=0, keepdims=True)
        hit = row == ik
        wts.append(jnp.sum(jnp.where(hit, scores, 0.0), axis=0, keepdims=True))
        cur = jnp.where(hit, neg, cur)
        chosen = jnp.where(hit, 1.0, chosen)
        picks.append(ik)
    wsum = wts[0]
    for k in range(1, TOP_K):
        wsum = wsum + wts[k]

    ti = lax.broadcasted_iota(I32, (tm, tm), 0)
    tj = lax.broadcasted_iota(I32, (tm, tm), 1)
    upper = (ti < tj).astype(BF16)
    chosen_b = chosen.astype(BF16)
    before = jnp.dot(chosen_b, upper, preferred_element_type=F32)
    total = jnp.dot(chosen_b, jnp.ones((tm, LANES), BF16), preferred_element_type=F32)
    base = carry[...]
    pos = before + jnp.concatenate([base] * (tm // LANES), axis=1)
    for k in range(TOP_K):
        hit = row == picks[k]
        idx_ref[k:k + 1, :] = picks[k]
        w_ref[k:k + 1, :] = wts[k] / wsum * ROUTED_SCALE
        rank_ref[k:k + 1, :] = jnp.sum(jnp.where(hit, pos, 0.0), axis=0, keepdims=True).astype(I32)
    carry[...] = base + total
    cnt_ref[...] = base + total


def _router(h2, rw_t, bias_col):
    t, d = h2.shape
    ne = rw_t.shape[0]
    tm = 256
    kt = pl.BlockSpec((TOP_K, tm), lambda i: (0, i))
    return pl.pallas_call(
        _router_kernel,
        grid=(t // tm,),
        in_specs=[pl.BlockSpec((tm, d), lambda i: (i, 0)),
                  pl.BlockSpec((ne, d), lambda i: (0, 0)),
                  pl.BlockSpec((ne, 1), lambda i: (0, 0))],
        out_specs=[kt, kt, kt, pl.BlockSpec((ne, LANES), lambda i: (0, 0))],
        out_shape=[jax.ShapeDtypeStruct((TOP_K, t), I32), jax.ShapeDtypeStruct((TOP_K, t), F32),
                   jax.ShapeDtypeStruct((TOP_K, t), I32), jax.ShapeDtypeStruct((ne, LANES), F32)],
        scratch_shapes=[pltpu.VMEM((ne, LANES), F32)],
        compiler_params=_params("arbitrary"),
        name="router",
    )(h2, rw_t, bias_col)


def _dest_kernel(start_ref, idx_ref, rank_ref, dest_ref):
    ne = start_ref.shape[0]
    idx = idx_ref[...]

    def body(e, acc):
        return jnp.where(idx == e, start_ref[e], acc)

    dest = rank_ref[...] + lax.fori_loop(0, ne, body, jnp.zeros(idx.shape, I32))
    for j in range(dest_ref.shape[0]):
        dest_ref[j] = dest[:, j * DEST_TOKENS:(j + 1) * DEST_TOKENS]


def _dest_rows(starts, idx_t, rank_t):
    t = idx_t.shape[1]
    tb = 2048
    per = tb // DEST_TOKENS
    return pl.pallas_call(
        _dest_kernel,
        grid_spec=pltpu.PrefetchScalarGridSpec(
            num_scalar_prefetch=1,
            grid=(t // tb,),
            in_specs=[pl.BlockSpec((TOP_K, tb), lambda i, s: (0, i)),
                      pl.BlockSpec((TOP_K, tb), lambda i, s: (0, i))],
            out_specs=pl.BlockSpec((per, TOP_K, DEST_TOKENS), lambda i, s: (i, 0, 0)),
        ),
        out_shape=jax.ShapeDtypeStruct((t // DEST_TOKENS, TOP_K, DEST_TOKENS), I32),
        compiler_params=_params("arbitrary"),
        name="dest_rows",
    )(starts, idx_t, rank_t)


def _dispatch_kernel(dest_hbm, h_ref, xs_hbm, idx_smem, isem, dsem):
    i = pl.program_id(0)
    n = pl.num_programs(0)
    tm = h_ref.shape[0]
    slot = i % 2

    def idx_copy(tile, s):
        return pltpu.make_async_copy(dest_hbm.at[tile], idx_smem.at[s], isem.at[s])

    @pl.when(i == 0)
    def _():
        idx_copy(0, 0).start()

    idx_copy(i, slot).wait()

    @pl.when(i + 1 < n)
    def _():
        idx_copy(i + 1, 1 - slot).start()

    def body(t, c):
        for k in range(TOP_K):
            d = idx_smem[slot, k, t]
            pltpu.make_async_copy(h_ref.at[pl.ds(t, 1), :], xs_hbm.at[pl.ds(d, 1), :], dsem).start(priority=k % 2)
        return c

    lax.fori_loop(0, tm, body, 0)
    for k in range(TOP_K):
        pltpu.make_async_copy(h_ref, xs_hbm.at[pl.ds(0, tm), :], dsem).wait()


def _dispatch(dest, h2, n_rows):
    t, d = h2.shape
    tm = DEST_TOKENS
    return pl.pallas_call(
        _dispatch_kernel,
        grid=(t // tm,),
        in_specs=[pl.BlockSpec(memory_space=pl.ANY), pl.BlockSpec((tm, d), lambda i: (i, 0))],
        out_specs=pl.BlockSpec(memory_space=pl.ANY),
        out_shape=jax.ShapeDtypeStruct((n_rows, d), F32),
        scratch_shapes=[pltpu.SMEM((2, TOP_K, tm), I32), pltpu.SemaphoreType.DMA((2,)),
                        pltpu.SemaphoreType.DMA(())],
        compiler_params=_params("arbitrary"),
        name="dispatch",
    )(dest, h2)


def _experts_kernel(e_ref, b_ref, lo_ref, hi_ref, nxt_ref, n_ref, xs_ref, w1_hbm, w3_hbm, w2_hbm, ys_ref,
                    w1s, w3s, w2s, w1b, w3b, w2b, wsem):
    w = pl.program_id(0)
    prev = jnp.maximum(w - 1, 0)

    def fetch(e):
        return (pltpu.make_async_copy(w1_hbm.at[e], w1s, wsem.at[0]),
                pltpu.make_async_copy(w3_hbm.at[e], w3s, wsem.at[1]),
                pltpu.make_async_copy(w2_hbm.at[e], w2s, wsem.at[2]))

    @pl.when(w < n_ref[0])
    def _():
        @pl.when(w == 0)
        def _():
            for cp in fetch(e_ref[0]):
                cp.start()

        @pl.when(jnp.logical_or(w == 0, e_ref[w] != e_ref[prev]))
        def _():
            for cp in fetch(e_ref[w]):
                cp.wait()
            w1b[...] = w1s[...].astype(BF16)
            w3b[...] = w3s[...].astype(BF16)
            w2b[...] = w2s[...].astype(BF16)

            @pl.when(nxt_ref[w] >= 0)
            def _():
                for cp in fetch(nxt_ref[w]):
                    cp.start()

        x = xs_ref[...].astype(BF16)
        h1 = jnp.dot(x, w1b[...], preferred_element_type=F32)
        h3 = jnp.dot(x, w3b[...], preferred_element_type=F32)
        act = (_silu(h1) * h3).astype(BF16)
        y = jnp.dot(act, w2b[...], preferred_element_type=F32)
        row = b_ref[w] * EXPERT_ROWS + lax.broadcasted_iota(I32, (EXPERT_ROWS, 1), 0)
        mine = jnp.logical_and(row >= lo_ref[w], row < hi_ref[w])
        new_block = jnp.logical_or(w == 0, b_ref[w] != b_ref[prev])

        @pl.when(new_block)
        def _():
            ys_ref[...] = jnp.where(mine, y, 0.0)

        @pl.when(jnp.logical_not(new_block))
        def _():
            ys_ref[...] = jnp.where(mine, y, ys_ref[...])


def _experts(item_expert, item_block, item_lo, item_hi, item_next, n_items, xs, w1, w3, w2):
    n_rows, d = xs.shape
    de = w1.shape[2]
    rows = lambda w, e, b, lo, hi, nx, n: (b[w], 0)
    hbm = pl.BlockSpec(memory_space=pl.ANY)
    return pl.pallas_call(
        _experts_kernel,
        grid_spec=pltpu.PrefetchScalarGridSpec(
            num_scalar_prefetch=6,
            grid=(item_expert.shape[0],),
            in_specs=[pl.BlockSpec((EXPERT_ROWS, d), rows), hbm, hbm, hbm],
            out_specs=pl.BlockSpec((EXPERT_ROWS, d), rows),
            scratch_shapes=[pltpu.VMEM((d, de), F32), pltpu.VMEM((d, de), F32), pltpu.VMEM((de, d), F32),
                            pltpu.VMEM((d, de), BF16), pltpu.VMEM((d, de), BF16), pltpu.VMEM((de, d), BF16),
                            pltpu.SemaphoreType.DMA((3,))],
        ),
        out_shape=jax.ShapeDtypeStruct((n_rows, d), F32),
        compiler_params=_params("arbitrary"),
        name="experts",
    )(item_expert, item_block, item_lo, item_hi, item_next, n_items, xs, w1, w3, w2)


def _shared_kernel(h_ref, w1_ref, w3_ref, w2_ref, y_ref):
    x = h_ref[...].astype(BF16)
    h1 = jnp.dot(x, w1_ref[...], preferred_element_type=F32)
    h3 = jnp.dot(x, w3_ref[...], preferred_element_type=F32)
    y_ref[...] = jnp.dot((_silu(h1) * h3).astype(BF16), w2_ref[...], preferred_element_type=F32)


def _shared(h2, w1_b, w3_b, w2_b):
    t, d = h2.shape
    de = w1_b.shape[1]
    tm = 512
    row = pl.BlockSpec((tm, d), lambda i: (i, 0))
    return pl.pallas_call(
        _shared_kernel,
        grid=(t // tm,),
        in_specs=[row, pl.BlockSpec((d, de), lambda i: (0, 0)), pl.BlockSpec((d, de), lambda i: (0, 0)),
                  pl.BlockSpec((de, d), lambda i: (0, 0))],
        out_specs=row,
        out_shape=jax.ShapeDtypeStruct((t, d), F32),
        compiler_params=_params("arbitrary"),
        name="shared_expert",
    )(h2, w1_b, w3_b, w2_b)


def _combine_kernel(dest_hbm, ys_hbm, x1_ref, ysh_ref, wt_ref, gate_ref, fg_ref, o_ref,
                    rows, idx_smem, isem, gsem):
    i = pl.program_id(0)
    n = pl.num_programs(0)
    tm = x1_ref.shape[0]
    slot = i % 2

    def idx_copy(tile, s):
        return pltpu.make_async_copy(dest_hbm.at[tile], idx_smem.at[s], isem.at[s])

    def issue_gathers(s):
        def body(t, c):
            for k in range(TOP_K):
                d = idx_smem[s, k, t]
                pltpu.make_async_copy(ys_hbm.at[pl.ds(d, 1), :], rows.at[s, k, pl.ds(t, 1), :],
                                      gsem.at[s]).start(priority=k % 2)
            return c

        lax.fori_loop(0, tm, body, 0)

    @pl.when(i == 0)
    def _():
        idx_copy(0, 0).start()
        idx_copy(0, 0).wait()
        issue_gathers(0)

        @pl.when(n > 1)
        def _():
            idx_copy(1, 1).start()

    @pl.when(i + 1 < n)
    def _():
        idx_copy(i + 1, 1 - slot).wait()
        issue_gathers(1 - slot)

    @pl.when(i + 2 < n)
    def _():
        idx_copy(i + 2, slot).start()

    for k in range(TOP_K):
        pltpu.make_async_copy(ys_hbm.at[pl.ds(0, tm), :], rows.at[slot, k], gsem.at[slot]).wait()

    w = wt_ref[...]
    acc = ysh_ref[...]
    for k in range(TOP_K):
        acc = acc + rows[slot, k] * w[:, k:k + 1]
    o_ref[...] = _rms(x1_ref[...] + gate_ref[...] * acc, fg_ref[...])


def _combine(dest, ys, x1, ysh, w_tok, gate2, final_g, seq):
    t, d = x1.shape
    tm = DEST_TOKENS
    per_b = seq // tm
    row = pl.BlockSpec((tm, d), lambda i: (i, 0))
    return pl.pallas_call(
        _combine_kernel,
        grid=(t // tm,),
        in_specs=[pl.BlockSpec(memory_space=pl.ANY), pl.BlockSpec(memory_space=pl.ANY), row, row,
                  pl.BlockSpec((tm, TOP_K), lambda i: (i, 0)),
                  pl.BlockSpec((None, 1, d), lambda i: (i // per_b, 0, 0)),
                  pl.BlockSpec((1, d), lambda i: (0, 0))],
        out_specs=row,
        out_shape=jax.ShapeDtypeStruct((t, d), F32),
        scratch_shapes=[pltpu.VMEM((2, TOP_K, tm, d), F32), pltpu.SMEM((2, TOP_K, tm), I32),
                        pltpu.SemaphoreType.DMA((2,)), pltpu.SemaphoreType.DMA((2,))],
        compiler_params=_params("arbitrary"),
        name="combine",
    )(dest, ys, x1, ysh, w_tok, gate2, final_g)


def _mixer(x2, mod6, cos_t, sin_t, bsz, seq, p):
    t, d = x2.shape
    shift1, scale1, gate1, shift2, scale2, _ = mod6
    d_rnn = p["conv_w"].shape[1]
    att_width = len(DILATION_GROUPS) * GROUP_COLS
    q_col = 2 * d_rnn
    gate_col = 2 * d_rnn + 3 * att_width
    w_in_b = p["w_in"].astype(BF16)

    h1 = _norm_mod(x2, p["norm1_g"].reshape(1, d), shift1, scale1, seq)
    rnn = _proj_act(h1, w_in_b, 0, 2 * d_rnn, d_rnn // GROUP_COLS, _gelu_tanh, "proj_rnn")
    gates = _proj_act(h1, w_in_b, gate_col, 2 * d, 0, jax.nn.sigmoid, "proj_gates")
    ya = _rglru(rnn, p["conv_w"], p["conv_b"].reshape(1, d_rnn),
                p["rg_wa"].astype(BF16), p["rg_ba"].reshape(1, d_rnn),
                p["rg_wi"].astype(BF16), p["rg_bi"].reshape(1, d_rnn),
                p["rg_lambda"].reshape(1, d_rnn), bsz, seq, d_rnn)

    outs, lses = [], []
    for g, (window, dilation) in enumerate(DILATION_GROUPS):
        qkv = _proj_qkv(h1, w_in_b, cos_t, sin_t, g, dilation, bsz, seq, q_col, att_width)
        o, l = _attention_group(qkv, g, window, dilation)
        outs.append(o)
        lses.append(l)

    merged = _merge(ya, outs, lses, gates, p["w_proj_rnn"].astype(BF16), p["w_proj_attn"].astype(BF16), seq)
    return _out_proj(merged, p["w_out"].astype(BF16), x2, gate1, p["norm2_g"].reshape(1, d),
                     shift2, scale2, seq)


def _moe(h2, p):
    t, d = h2.shape
    ne = p["router_w"].shape[1]
    idx_t, w_t, rank_t, cnt = _router(h2, p["router_w"].T.astype(BF16), p["router_bias"].reshape(ne, 1))

    counts = cnt[:, 0].astype(I32)
    ends = jnp.cumsum(counts).astype(I32)
    starts = ends - counts
    n_rows = t * TOP_K
    first_blk = starts // EXPERT_ROWS
    n_blk_e = jnp.where(counts > 0, (ends - 1) // EXPERT_ROWS - first_blk + 1, 0)
    item_end = jnp.cumsum(n_blk_e).astype(I32)
    item_start = item_end - n_blk_e
    n_items = item_end[-1]
    max_items = n_rows // EXPERT_ROWS + ne
    w = jnp.minimum(jnp.arange(max_items, dtype=I32), n_items - 1)
    item_expert = jnp.minimum(jnp.searchsorted(item_end, w, side="right"), ne - 1).astype(I32)
    item_block = first_blk[item_expert] + (w - item_start[item_expert])
    after = item_end[item_expert]
    item_next = jnp.where(after < n_items, item_expert[jnp.minimum(after, max_items - 1)], -1).astype(I32)

    dest = _dest_rows(starts, idx_t, rank_t)
    xs = _dispatch(dest, h2, n_rows)
    ys = _experts(item_expert, item_block, starts[item_expert], ends[item_expert], item_next,
                  n_items.reshape(1), xs, p["exp_w1"], p["exp_w3"], p["exp_w2"])
    ysh = _shared(h2, p["sh_w1"].astype(BF16), p["sh_w3"].astype(BF16), p["sh_w2"].astype(BF16))
    return dest, ys, ysh, w_t.T


def kernel(x, c, positions, ada_w, ada_b, norm1_g, w_in, conv_w, conv_b, rg_wa, rg_ba, rg_wi, rg_bi, rg_lambda, w_proj_rnn, w_proj_attn, w_out, norm2_g, router_w, router_bias, exp_w1, exp_w3, exp_w2, sh_w1, sh_w3, sh_w2, final_g):
    bsz, seq, d = x.shape
    assert ada_w.shape[0] == 1, "the fused final norm assumes a single layer"
    t = bsz * seq
    x2 = x.reshape(t, d)
    first = lambda a: a.reshape(a.shape[1:])

    half = HEAD_DIM // 2
    inv_freq = ROPE_THETA ** (-jnp.arange(half, dtype=F32) * 2.0 / HEAD_DIM)
    freq = jnp.concatenate([inv_freq, inv_freq]).reshape(1, HEAD_DIM)
    sign = jnp.concatenate([-jnp.ones((half,), F32), jnp.ones((half,), F32)]).reshape(1, HEAD_DIM)
    cos_t, sin_t = _rope_tables(positions.reshape(t, 1), freq, sign)

    c_pad = jnp.zeros((SUBLANES, d), F32).at[:bsz].set(c)
    mod = _ada_mod(c_pad, first(ada_w), ada_b.reshape(1, -1))
    mod6 = tuple(mod[:bsz, k * d:(k + 1) * d].reshape(bsz, 1, d) for k in range(6))

    p = dict(norm1_g=first(norm1_g), w_in=first(w_in), conv_w=first(conv_w), conv_b=first(conv_b),
             rg_wa=first(rg_wa), rg_ba=first(rg_ba), rg_wi=first(rg_wi), rg_bi=first(rg_bi),
             rg_lambda=first(rg_lambda), w_proj_rnn=first(w_proj_rnn), w_proj_attn=first(w_proj_attn),
             w_out=first(w_out), norm2_g=first(norm2_g), router_w=first(router_w),
             router_bias=first(router_bias), exp_w1=first(exp_w1), exp_w3=first(exp_w3),
             exp_w2=first(exp_w2), sh_w1=first(sh_w1), sh_w3=first(sh_w3), sh_w2=first(sh_w2))
    x1, h2 = _mixer(x2, mod6, cos_t, sin_t, bsz, seq, p)
    dest, ys, ysh, w_tok = _moe(h2, p)
    out = _combine(dest, ys, x1, ysh, w_tok, mod6[5], final_g.reshape(1, d), seq)
    return out.reshape(bsz, seq, d)
```

```python
import functools
import math

import jax
import jax.numpy as jnp
import numpy as np
from jax import lax
from jax.experimental import pallas as pl
from jax.experimental.pallas import tpu as pltpu

F32 = jnp.float32
BF16 = jnp.bfloat16
I32 = jnp.int32

HEAD_DIM = 128
HEADS_PER_GROUP = 4
DILATION_GROUPS = ((128, 1), (512, 4), (2048, 16))
ROPE_THETA = 10000.0
CONV_WIDTH = 4
LRU_C = 8.0
TOP_K = 8
N_EXPERT_GROUPS = 8
TOPK_GROUPS = 4
ROUTED_SCALE = 2.5
NORM_EPS = 1e-6

LANES = 128
SUBLANES = 8
VMEM_LIMIT_BYTES = 56 * 1024 * 1024

GROUP_COLS = HEADS_PER_GROUP * HEAD_DIM
EXPERT_ROWS = 256
DEST_TOKENS = 128


def _params(*sem):
    return pltpu.CompilerParams(dimension_semantics=sem, vmem_limit_bytes=VMEM_LIMIT_BYTES)


def _gelu_tanh(x):
    return 0.5 * x * (1.0 + jnp.tanh(math.sqrt(2.0 / math.pi) * (x + 0.044715 * (x * x * x))))


def _silu(x):
    return x * jax.nn.sigmoid(x)


def _rms(x, g):
    ms = jnp.mean(x * x, axis=-1, keepdims=True)
    return x * lax.rsqrt(ms + NORM_EPS) * g


def _ada_kernel(c_ref, w_ref, b_ref, o_ref):
    a = _silu(c_ref[...]).astype(BF16)
    o_ref[...] = jnp.dot(a, w_ref[...].astype(BF16), preferred_element_type=F32) + b_ref[...]


def _ada_mod(c_pad, ada_w, ada_b):
    rows, d = c_pad.shape
    n = ada_w.shape[1]
    tn = 1024
    return pl.pallas_call(
        _ada_kernel,
        grid=(n // tn,),
        in_specs=[
            pl.BlockSpec((rows, d), lambda j: (0, 0)),
            pl.BlockSpec((d, tn), lambda j: (0, j)),
            pl.BlockSpec((1, tn), lambda j: (0, j)),
        ],
        out_specs=pl.BlockSpec((rows, tn), lambda j: (0, j)),
        out_shape=jax.ShapeDtypeStruct((rows, n), F32),
        compiler_params=_params("arbitrary"),
        name="ada_mod",
    )(c_pad, ada_w, ada_b)


def _rope_kernel(pos_ref, freq_ref, sign_ref, cos_ref, sin_ref):
    ang = pos_ref[...].astype(F32) * freq_ref[...]
    cos_ref[...] = jnp.cos(ang)
    sin_ref[...] = jnp.sin(ang) * sign_ref[...]


def _rope_tables(pos_col, freq, sign):
    t = pos_col.shape[0]
    tm = 1024
    return pl.pallas_call(
        _rope_kernel,
        grid=(t // tm,),
        in_specs=[
            pl.BlockSpec((tm, 1), lambda i: (i, 0)),
            pl.BlockSpec((1, HEAD_DIM), lambda i: (0, 0)),
            pl.BlockSpec((1, HEAD_DIM), lambda i: (0, 0)),
        ],
        out_specs=[pl.BlockSpec((tm, HEAD_DIM), lambda i: (i, 0))] * 2,
        out_shape=[jax.ShapeDtypeStruct((t, HEAD_DIM), F32)] * 2,
        compiler_params=_params("arbitrary"),
        name="rope_tables",
    )(pos_col, freq, sign)


def _norm_kernel(x_ref, g_ref, sh_ref, sc_ref, h_ref):
    h_ref[...] = (_rms(x_ref[...], g_ref[...]) * (1.0 + sc_ref[...]) + sh_ref[...]).astype(h_ref.dtype)


def _norm_mod(x2, g, shift, scale, seq):
    t, d = x2.shape
    tm = 1024
    per_b = seq // tm
    bvec = pl.BlockSpec((None, 1, d), lambda i: (i // per_b, 0, 0))
    return pl.pallas_call(
        _norm_kernel,
        grid=(t // tm,),
        in_specs=[pl.BlockSpec((tm, d), lambda i: (i, 0)), pl.BlockSpec((1, d), lambda i: (0, 0)), bvec, bvec],
        out_specs=pl.BlockSpec((tm, d), lambda i: (i, 0)),
        out_shape=jax.ShapeDtypeStruct((t, d), BF16),
        compiler_params=_params("arbitrary"),
        name="norm_mod",
    )(x2, g, shift, scale)


def _proj_act_kernel(h_ref, w_ref, o_ref, *, plain_tiles, act):
    acc = jnp.dot(h_ref[...], w_ref[...], preferred_element_type=F32)
    j = pl.program_id(1)

    @pl.when(j < plain_tiles)
    def _():
        o_ref[...] = acc.astype(o_ref.dtype)

    @pl.when(j >= plain_tiles)
    def _():
        o_ref[...] = act(acc).astype(o_ref.dtype)


def _proj_act(h1, w_in_b, col0, ncols, plain_tiles, act, name):
    t, d = h1.shape
    tm, tn = 1024, GROUP_COLS
    j0 = col0 // tn
    return pl.pallas_call(
        functools.partial(_proj_act_kernel, plain_tiles=plain_tiles, act=act),
        grid=(t // tm, ncols // tn),
        in_specs=[pl.BlockSpec((tm, d), lambda i, j: (i, 0)),
                  pl.BlockSpec((d, tn), lambda i, j: (0, j0 + j))],
        out_specs=pl.BlockSpec((tm, tn), lambda i, j: (i, j)),
        out_shape=jax.ShapeDtypeStruct((t, ncols), BF16),
        compiler_params=_params("arbitrary", "arbitrary"),
        name=name,
    )(h1, w_in_b)


def _proj_qkv_kernel(h_ref, w_ref, cos_ref, sin_ref, o_ref, scr, *, dilation):
    tm = h_ref.shape[0]
    acc = jnp.dot(h_ref[...], w_ref[...], preferred_element_type=F32)
    j = pl.program_id(1)

    @pl.when(j < 2)
    def _():
        scale = jnp.where(j == 0, HEAD_DIM ** -0.5, 1.0).astype(F32)
        c = cos_ref[...] * scale
        s = sin_ref[...] * scale
        for h in range(HEADS_PER_GROUP):
            v = acc[:, h * HEAD_DIM:(h + 1) * HEAD_DIM]
            scr[h] = v * c + pltpu.roll(v, HEAD_DIM // 2, axis=1) * s

    @pl.when(j == 2)
    def _():
        for h in range(HEADS_PER_GROUP):
            scr[h] = acc[:, h * HEAD_DIM:(h + 1) * HEAD_DIM]

    sub = tm // dilation
    for r in range(dilation):
        for h in range(HEADS_PER_GROUP):
            rows = scr[h] if dilation == 1 else scr[h, pl.ds(r, sub, stride=dilation), :]
            o_ref[r, :, h * HEAD_DIM:(h + 1) * HEAD_DIM] = rows.astype(o_ref.dtype)


def _proj_qkv(h1, w_in_b, cos_t, sin_t, g, dilation, bsz, seq, q_col, att_width):
    t, d = h1.shape
    tm, tn = 1024, GROUP_COLS
    per_b = seq // tm
    j0 = q_col // tn + g
    step = att_width // tn
    sub = tm // dilation
    return pl.pallas_call(
        functools.partial(_proj_qkv_kernel, dilation=dilation),
        grid=(t // tm, 3),
        in_specs=[pl.BlockSpec((tm, d), lambda i, j: (i, 0)),
                  pl.BlockSpec((d, tn), lambda i, j: (0, j0 + step * j)),
                  pl.BlockSpec((tm, HEAD_DIM), lambda i, j: (i, 0)),
                  pl.BlockSpec((tm, HEAD_DIM), lambda i, j: (i, 0))],
        out_specs=pl.BlockSpec((None, dilation, sub, tn), lambda i, j: (i // per_b, 0, i % per_b, j)),
        out_shape=jax.ShapeDtypeStruct((bsz, dilation, seq // dilation, 3 * tn), BF16),
        scratch_shapes=[pltpu.VMEM((HEADS_PER_GROUP, tm, HEAD_DIM), F32)],
        compiler_params=_params("arbitrary", "arbitrary"),
        name=f"proj_qkv_g{g}",
    )(h1, w_in_b, cos_t, sin_t)


def _rglru_kernel(xr_ref, gr_ref, cw_ref, cb_ref, wa_ref, ba_ref, wi_ref, bi_ref, lam_ref,
                  ya_ref, xbuf, a_scr, b_scr, hcar):
    tt = xr_ref.shape[0]
    halo = SUBLANES

    @pl.when(pl.program_id(2) == 0)
    def _():
        xbuf[0:halo, :] = jnp.zeros((halo, LANES), F32)
        hcar[...] = jnp.zeros_like(hcar)

    xbuf[halo:halo + tt, :] = xr_ref[...].astype(F32)
    u = cb_ref[...] + cw_ref[CONV_WIDTH - 1:CONV_WIDTH, :] * xbuf[halo:halo + tt, :]
    for j in range(CONV_WIDTH - 1):
        s = CONV_WIDTH - 1 - j
        u = u + cw_ref[j:j + 1, :] * xbuf[halo - s:halo - s + tt, :]
    xbuf[0:halo, :] = xbuf[tt:tt + halo, :]

    ub = u.astype(BF16)
    r = 0.5 + 0.5 * jnp.tanh(0.5 * (jnp.dot(ub, wa_ref[...], preferred_element_type=F32) + ba_ref[...]))
    ig = 0.5 + 0.5 * jnp.tanh(0.5 * (jnp.dot(ub, wi_ref[...], preferred_element_type=F32) + bi_ref[...]))
    z = -lam_ref[...]
    softplus = jnp.maximum(z, 0.0) + jnp.log1p(jnp.exp(-jnp.abs(z)))
    t = jnp.tanh((-0.5 * LRU_C) * r * softplus)
    q = 1.0 / (1.0 - t)
    a = (1.0 + t) * q
    b = (2.0 * q) * jnp.sqrt(-t) * (ig * u)

    seg = tt // SUBLANES
    pitch = a_scr.shape[0] // SUBLANES
    for s in range(SUBLANES):
        a_scr[s * pitch:s * pitch + seg, :] = a[s * seg:(s + 1) * seg, :]
        b_scr[s * pitch:s * pitch + seg, :] = b[s * seg:(s + 1) * seg, :]
    h = jnp.zeros((SUBLANES, LANES), F32)
    prod = jnp.ones((SUBLANES, LANES), F32)
    for j in range(seg):
        rows_j = pl.ds(j, SUBLANES, stride=pitch)
        aj = a_scr[rows_j, :]
        h = aj * h + b_scr[rows_j, :]
        prod = aj * prod
        b_scr[rows_j, :] = h
        a_scr[rows_j, :] = prod
    c = hcar[0:1, :]
    entering = []
    for s in range(SUBLANES):
        entering.append(c)
        c = h[s:s + 1, :] + prod[s:s + 1, :] * c
    hcar[...] = jnp.broadcast_to(c, hcar.shape)
    start = jnp.concatenate(entering, axis=0)
    for j in range(seg):
        rows_j = pl.ds(j, SUBLANES, stride=pitch)
        b_scr[rows_j, :] = b_scr[rows_j, :] + a_scr[rows_j, :] * start
    for s in range(SUBLANES):
        rs = slice(s * seg, (s + 1) * seg)
        ya_ref[rs, :] = (b_scr[s * pitch:s * pitch + seg, :] * gr_ref[rs, :].astype(F32)).astype(ya_ref.dtype)


def _rglru(rnn, conv_w, conv_b, wa_b, ba, wi_b, bi, lam, bsz, seq, d_rnn):
    t = rnn.shape[0]
    tt = 512
    nct = d_rnn // LANES
    per_b = seq // tt
    row = lambda b, c, s: (b * per_b + s, c)
    vec = lambda b, c, s: (0, c)
    return pl.pallas_call(
        _rglru_kernel,
        grid=(bsz, nct, per_b),
        in_specs=[
            pl.BlockSpec((tt, LANES), row),
            pl.BlockSpec((tt, LANES), lambda b, c, s: (b * per_b + s, nct + c)),
            pl.BlockSpec((CONV_WIDTH, LANES), vec),
            pl.BlockSpec((1, LANES), vec),
            pl.BlockSpec((None, LANES, LANES), lambda b, c, s: (c, 0, 0)),
            pl.BlockSpec((1, LANES), vec),
            pl.BlockSpec((None, LANES, LANES), lambda b, c, s: (c, 0, 0)),
            pl.BlockSpec((1, LANES), vec),
            pl.BlockSpec((1, LANES), vec),
        ],
        out_specs=pl.BlockSpec((tt, LANES), row),
        out_shape=jax.ShapeDtypeStruct((t, d_rnn), BF16),
        scratch_shapes=[pltpu.VMEM((tt + SUBLANES, LANES), F32), pltpu.VMEM((tt + SUBLANES * SUBLANES, LANES), F32),
                        pltpu.VMEM((tt + SUBLANES * SUBLANES, LANES), F32), pltpu.VMEM((SUBLANES, LANES), F32)],
        compiler_params=_params("arbitrary", "arbitrary", "arbitrary"),
        name="rglru",
    )(rnn, rnn, conv_w, conv_b, wa_b, ba, wi_b, bi, lam)


ATTN_BLOCKS = 4


def _attn_kernel(q_ref, kc_ref, kp_ref, vc_ref, vp_ref, o_ref, l_ref, *, blk):
    nq = q_ref.shape[0] // blk
    not_first = pl.program_id(2) > 0
    qi = lax.broadcasted_iota(I32, (blk, blk), 0)
    kj = lax.broadcasted_iota(I32, (blk, blk), 1)
    tri_prev = kj >= qi
    mask_cur = kj <= qi
    nt = (((1,), (1,)), ((), ()))
    for j in range(nq):
        rs = slice(j * blk, (j + 1) * blk)
        ps = slice((j - 1) * blk, j * blk)
        for h in range(HEADS_PER_GROUP):
            cs = slice(h * HEAD_DIM, (h + 1) * HEAD_DIM)
            q = q_ref[rs, cs]
            if j == 0:
                kp, vp, mask_prev = kp_ref[:, cs], vp_ref[:, cs], jnp.logical_and(tri_prev, not_first)
            else:
                kp, vp, mask_prev = kc_ref[ps, cs], vc_ref[ps, cs], tri_prev
            sp = lax.dot_general(q, kp, nt, preferred_element_type=F32)
            sc = lax.dot_general(q, kc_ref[rs, cs], nt, preferred_element_type=F32)
            sp = jnp.where(mask_prev, sp, -jnp.inf)
            sc = jnp.where(mask_cur, sc, -jnp.inf)
            m = jnp.maximum(jnp.max(sp, axis=-1, keepdims=True), jnp.max(sc, axis=-1, keepdims=True))
            pp = jnp.exp(sp - m)
            pc = jnp.exp(sc - m)
            den = jnp.sum(pp, axis=-1, keepdims=True) + jnp.sum(pc, axis=-1, keepdims=True)
            out = (jnp.dot((pp / den).astype(BF16), vp, preferred_element_type=F32)
                   + jnp.dot((pc / den).astype(BF16), vc_ref[rs, cs], preferred_element_type=F32))
            o_ref[rs, cs] = out
            l_ref[rs, cs] = jnp.broadcast_to(m + jnp.log(den), (blk, HEAD_DIM))


def _attention_group(qkv, g, window, dilation):
    bsz, _, length, _ = qkv.shape
    blk = window // dilation
    nb = length // blk
    nq = math.gcd(ATTN_BLOCKS, nb)
    cur = lambda c: (lambda b, r, n: (b, r, n, c))
    prev = lambda c: (lambda b, r, n: (b, r, jnp.maximum(n * nq - 1, 0), c))
    spec = lambda f: pl.BlockSpec((None, None, nq * blk, GROUP_COLS), f)
    pspec = lambda f: pl.BlockSpec((None, None, blk, GROUP_COLS), f)
    out_sds = jax.ShapeDtypeStruct((bsz, dilation, length, GROUP_COLS), F32)
    return pl.pallas_call(
        functools.partial(_attn_kernel, blk=blk),
        grid=(bsz, dilation, nb // nq),
        in_specs=[spec(cur(0)), spec(cur(1)), pspec(prev(1)), spec(cur(2)), pspec(prev(2))],
        out_specs=[spec(cur(0)), spec(cur(0))],
        out_shape=[out_sds, out_sds],
        compiler_params=_params("arbitrary", "arbitrary", "arbitrary"),
        name=f"attn_g{g}",
    )(qkv, qkv, qkv, qkv, qkv)


def _merge_kernel(ya_ref, o0, o1, o2, l0, l1, l2, ga_ref, gb_ref, wr_ref, wa_ref, m_ref,
                  yb_scr, o_scr, l_scr):
    tm = ya_ref.shape[0]

    @pl.when(pl.program_id(1) == 0)
    def _():
        for g, (o_ref, l_ref) in enumerate(((o0, l0), (o1, l1), (o2, l2))):
            dil = o_ref.shape[0]
            for r in range(dil):
                for h in range(HEADS_PER_GROUP):
                    cs = slice(h * HEAD_DIM, (h + 1) * HEAD_DIM)
                    if dil == 1:
                        o_scr[g, h] = o_ref[r, :, cs]
                        l_scr[g, h] = l_ref[r, :, cs]
                    else:
                        o_scr[g, h, pl.ds(r, tm // dil, stride=dil), :] = o_ref[r, :, cs]
                        l_scr[g, h, pl.ds(r, tm // dil, stride=dil), :] = l_ref[r, :, cs]
        for h in range(HEADS_PER_GROUP):
            la, lb, lc = l_scr[0, h], l_scr[1, h], l_scr[2, h]
            m = jnp.maximum(jnp.maximum(la, lb), lc)
            ea, eb, ec = jnp.exp(la - m), jnp.exp(lb - m), jnp.exp(lc - m)
            tot = ea + eb + ec
            yb = (ea / tot) * o_scr[0, h] + (eb / tot) * o_scr[1, h] + (ec / tot) * o_scr[2, h]
            yb_scr[:, h * HEAD_DIM:(h + 1) * HEAD_DIM] = yb.astype(BF16)

    pa = jnp.dot(ya_ref[...], wr_ref[...], preferred_element_type=F32)
    pb = jnp.dot(yb_scr[...], wa_ref[...], preferred_element_type=F32)
    m_ref[...] = (ga_ref[...].astype(F32) * pa + gb_ref[...].astype(F32) * pb).astype(m_ref.dtype)


def _merge(ya, outs, lses, gates, wr_b, wa_b, seq):
    t, d_rnn = ya.shape
    d = wr_b.shape[1]
    tm, tn = 512, GROUP_COLS
    per_b = seq // tm
    n_groups = len(outs)

    def grp(o):
        dil = o.shape[1]
        return pl.BlockSpec((None, dil, tm // dil, GROUP_COLS), lambda i, j: (i // per_b, 0, i % per_b, 0))

    return pl.pallas_call(
        _merge_kernel,
        grid=(t // tm, d // tn),
        in_specs=[
            pl.BlockSpec((tm, d_rnn), lambda i, j: (i, 0)),
            *[grp(o) for o in outs], *[grp(l) for l in lses],
            pl.BlockSpec((tm, tn), lambda i, j: (i, j)),
            pl.BlockSpec((tm, tn), lambda i, j: (i, d // tn + j)),
            pl.BlockSpec((d_rnn, tn), lambda i, j: (0, j)),
            pl.BlockSpec((GROUP_COLS, tn), lambda i, j: (0, j)),
        ],
        out_specs=pl.BlockSpec((tm, tn), lambda i, j: (i, j)),
        out_shape=jax.ShapeDtypeStruct((t, d), BF16),
        scratch_shapes=[pltpu.VMEM((tm, GROUP_COLS), BF16),
                        pltpu.VMEM((n_groups, HEADS_PER_GROUP, tm, HEAD_DIM), F32),
                        pltpu.VMEM((n_groups, HEADS_PER_GROUP, tm, HEAD_DIM), F32)],
        compiler_params=_params("arbitrary", "arbitrary"),
        name="merge_proj",
    )(ya, *outs, *lses, gates, gates, wr_b, wa_b)


U32 = jnp.uint32
HIGH_HALF = np.uint32(0xFFFF0000)


def _token_rows(d):
    assert d % (2 * LANES) == 0
    return d // (2 * LANES)


def _pack_rows(v, dst_ref):
    rows, d = v.shape
    tr = _token_rows(d)
    lo = pltpu.bitcast(v[:, :d // 2].astype(BF16).astype(F32), U32)
    hi = pltpu.bitcast(v[:, d // 2:].astype(BF16).astype(F32), U32)
    word = (lo >> 16) | (hi & HIGH_HALF)
    for s in range(tr):
        dst_ref[pl.ds(s, rows, stride=tr), :] = word[:, s * LANES:(s + 1) * LANES]


def _unpack_slab(src_ref, s, rows, tr):
    word = src_ref[pl.ds(s, rows, stride=tr), :]
    return pltpu.bitcast(word << 16, F32), pltpu.bitcast(word & HIGH_HALF, F32)


def _outproj_kernel(m_ref, w_ref, x_ref, gate_ref, g2_ref, sh_ref, sc_ref, x1_ref, h2_ref, h2p_ref):
    x1 = x_ref[...] + gate_ref[...] * jnp.dot(m_ref[...], w_ref[...], preferred_element_type=F32)
    x1_ref[...] = x1
    h2 = _rms(x1, g2_ref[...]) * (1.0 + sc_ref[...]) + sh_ref[...]
    h2_ref[...] = h2.astype(h2_ref.dtype)
    _pack_rows(h2, h2p_ref)


def _out_proj(merged, w_out_b, x2, gate1, g2, shift2, scale2, seq):
    t, d = x2.shape
    tr = _token_rows(d)
    tm = 512
    per_b = seq // tm
    row = pl.BlockSpec((tm, d), lambda i: (i, 0))
    bvec = pl.BlockSpec((None, 1, d), lambda i: (i // per_b, 0, 0))
    return pl.pallas_call(
        _outproj_kernel,
        grid=(t // tm,),
        in_specs=[row, pl.BlockSpec((d, d), lambda i: (0, 0)), row, bvec,
                  pl.BlockSpec((1, d), lambda i: (0, 0)), bvec, bvec],
        out_specs=[row, row, pl.BlockSpec((tm * tr, LANES), lambda i: (i, 0))],
        out_shape=[jax.ShapeDtypeStruct((t, d), F32), jax.ShapeDtypeStruct((t, d), BF16),
                   jax.ShapeDtypeStruct((t * tr, LANES), U32)],
        compiler_params=_params("arbitrary"),
        name="out_proj",
    )(merged, w_out_b, x2, gate1, g2, shift2, scale2)


def _router_kernel(h_ref, rw_ref, bias_ref, idx_ref, w_ref, rank_ref, cnt_ref, carry):
    ne = rw_ref.shape[0]
    tm = h_ref.shape[0]
    gsz = ne // N_EXPERT_GROUPS

    @pl.when(pl.program_id(0) == 0)
    def _():
        carry[...] = jnp.zeros_like(carry)

    logits = lax.dot_general(rw_ref[...], h_ref[...].astype(BF16), (((1,), (1,)), ((), ())),
                             preferred_element_type=F32)
    scores = jax.nn.sigmoid(logits)
    sel = scores + bias_ref[...]
    row = lax.broadcasted_iota(I32, (ne, tm), 0)
    neg = -jnp.inf

    gscore = []
    rg = lax.broadcasted_iota(I32, (gsz, tm), 0)
    for g in range(N_EXPERT_GROUPS):
        sg = sel[g * gsz:(g + 1) * gsz, :]
        m1 = jnp.max(sg, axis=0, keepdims=True)
        i1 = jnp.min(jnp.where(sg == m1, rg, ne), axis=0, keepdims=True)
        m2 = jnp.max(jnp.where(rg == i1, neg, sg), axis=0, keepdims=True)
        gscore.append(m1 + m2)
    keep_rows = []
    for g in range(N_EXPERT_GROUPS):
        beaten = jnp.zeros((1, tm), I32)
        for o in range(N_EXPERT_GROUPS):
            if o == g:
                continue
            wins = (gscore[o] >= gscore[g]) if o < g else (gscore[o] > gscore[g])
            beaten = beaten + wins.astype(I32)
        keep_rows.append(jnp.broadcast_to(beaten, (gsz, tm)))
    cur = jnp.where(jnp.concatenate(keep_rows, axis=0) < TOPK_GROUPS, sel, neg)

    chosen = jnp.zeros((ne, tm), F32)
    picks, wts = [], []
    for _ in range(TOP_K):
        m = jnp.max(cur, axis=0, keepdims=True)
        ik = jnp.min(jnp.where(cur == m, row, ne), axis=0, keepdims=True)
        hit = row == ik
        wts.append(jnp.sum(jnp.where(hit, scores, 0.0), axis=0, keepdims=True))
        cur = jnp.where(hit, neg, cur)
        chosen = jnp.where(hit, 1.0, chosen)
        picks.append(ik)
    wsum = wts[0]
    for k in range(1, TOP_K):
        wsum = wsum + wts[k]

    ti = lax.broadcasted_iota(I32, (tm, tm), 0)
    tj = lax.broadcasted_iota(I32, (tm, tm), 1)
    upper = (ti < tj).astype(BF16)
    chosen_b = chosen.astype(BF16)
    before = jnp.dot(chosen_b, upper, preferred_element_type=F32)
    total = jnp.dot(chosen_b, jnp.ones((tm, LANES), BF16), preferred_element_type=F32)
    base = carry[...]
    pos = before + jnp.concatenate([base] * (tm // LANES), axis=1)
    for k in range(TOP_K):
        hit = row == picks[k]
        idx_ref[k:k + 1, :] = picks[k]
        w_ref[k:k + 1, :] = wts[k] / wsum * ROUTED_SCALE
        rank_ref[k:k + 1, :] = jnp.sum(jnp.where(hit, pos, 0.0), axis=0, keepdims=True).astype(I32)
    carry[...] = base + total
    cnt_ref[...] = base + total


def _router(h2, rw_t, bias_col):
    t, d = h2.shape
    ne = rw_t.shape[0]
    tm = 256
    kt = pl.BlockSpec((TOP_K, tm), lambda i: (0, i))
    return pl.pallas_call(
        _router_kernel,
        grid=(t // tm,),
        in_specs=[pl.BlockSpec((tm, d), lambda i: (i, 0)),
                  pl.BlockSpec((ne, d), lambda i: (0, 0)),
                  pl.BlockSpec((ne, 1), lambda i: (0, 0))],
        out_specs=[kt, kt, kt, pl.BlockSpec((ne, LANES), lambda i: (0, 0))],
        out_shape=[jax.ShapeDtypeStruct((TOP_K, t), I32), jax.ShapeDtypeStruct((TOP_K, t), F32),
                   jax.ShapeDtypeStruct((TOP_K, t), I32), jax.ShapeDtypeStruct((ne, LANES), F32)],
        scratch_shapes=[pltpu.VMEM((ne, LANES), F32)],
        compiler_params=_params("arbitrary"),
        name="router",
    )(h2, rw_t, bias_col)


def _dest_kernel(start_ref, idx_ref, rank_ref, dest_ref):
    ne = start_ref.shape[0]
    idx = idx_ref[...]

    def body(e, acc):
        return jnp.where(idx == e, start_ref[e], acc)

    dest = rank_ref[...] + lax.fori_loop(0, ne, body, jnp.zeros(idx.shape, I32))
    for j in range(dest_ref.shape[0]):
        dest_ref[j] = dest[:, j * DEST_TOKENS:(j + 1) * DEST_TOKENS]


def _dest_rows(starts, idx_t, rank_t):
    t = idx_t.shape[1]
    tb = 2048
    per = tb // DEST_TOKENS
    return pl.pallas_call(
        _dest_kernel,
        grid_spec=pltpu.PrefetchScalarGridSpec(
            num_scalar_prefetch=1,
            grid=(t // tb,),
            in_specs=[pl.BlockSpec((TOP_K, tb), lambda i, s: (0, i)),
                      pl.BlockSpec((TOP_K, tb), lambda i, s: (0, i))],
            out_specs=pl.BlockSpec((per, TOP_K, DEST_TOKENS), lambda i, s: (i, 0, 0)),
        ),
        out_shape=jax.ShapeDtypeStruct((t // DEST_TOKENS, TOP_K, DEST_TOKENS), I32),
        compiler_params=_params("arbitrary"),
        name="dest_rows",
    )(starts, idx_t, rank_t)


ROW_DMA_UNROLL = 4
DEST_PER_TILE = DEST_TOKENS * TOP_K


def _idx_copy(dest_hbm, idx_smem, isem, tile, s):
    return pltpu.make_async_copy(
        dest_hbm.at[tile], idx_smem.at[pl.ds(pl.multiple_of(s * DEST_PER_TILE, DEST_PER_TILE), DEST_PER_TILE)],
        isem.at[s])


def _token_rows_at(ref, token, tr):
    return ref.at[pl.ds(pl.multiple_of(token * tr, tr), tr), :]


def _dispatch_kernel(dest_hbm, h_ref, xs_hbm, idx_smem, isem, dsem, *, tr):
    i = pl.program_id(0)
    n = pl.num_programs(0)
    tm = h_ref.shape[0] // tr
    slot = i % 2
    idx_copy = functools.partial(_idx_copy, dest_hbm, idx_smem, isem)

    @pl.when(i == 0)
    def _():
        idx_copy(0, 0).start()

    idx_copy(i, slot).wait()

    @pl.when(i + 1 < n)
    def _():
        idx_copy(i + 1, 1 - slot).start()

    base = slot * DEST_PER_TILE

    def body(c, carry):
        for u in range(ROW_DMA_UNROLL):
            t = c * ROW_DMA_UNROLL + u
            for k in range(TOP_K):
                d = idx_smem[base + t * TOP_K + k]
                pltpu.make_async_copy(_token_rows_at(h_ref, t, tr), _token_rows_at(xs_hbm, d, tr),
                                      dsem).start(priority=k % 2)
        return carry

    lax.fori_loop(0, tm // ROW_DMA_UNROLL, body, 0)
    for k in range(TOP_K):
        pltpu.make_async_copy(h_ref, xs_hbm.at[pl.ds(0, tm * tr), :], dsem).wait()


def _dispatch(dest, h2p, tr):
    t = h2p.shape[0] // tr
    tm = DEST_TOKENS
    return pl.pallas_call(
        functools.partial(_dispatch_kernel, tr=tr),
        grid=(t // tm,),
        in_specs=[pl.BlockSpec(memory_space=pl.ANY), pl.BlockSpec((tm * tr, LANES), lambda i: (i, 0))],
        out_specs=pl.BlockSpec(memory_space=pl.ANY),
        out_shape=jax.ShapeDtypeStruct((t * TOP_K * tr, LANES), U32),
        scratch_shapes=[pltpu.SMEM((2 * DEST_PER_TILE,), I32), pltpu.SemaphoreType.DMA((2,)),
                        pltpu.SemaphoreType.DMA(())],
        compiler_params=_params("arbitrary"),
        name="dispatch",
    )(dest, h2p)


def _experts_kernel(e_ref, b_ref, lo_ref, hi_ref, nxt_ref, n_ref, xs_ref, w1_hbm, w3_hbm, w2_hbm, ys_ref,
                    w1s, w3s, w2s, w1b, w3b, w2b, xb, yp, wsem):
    w = pl.program_id(0)
    prev = jnp.maximum(w - 1, 0)

    def fetch(e):
        return (pltpu.make_async_copy(w1_hbm.at[e], w1s, wsem.at[0]),
                pltpu.make_async_copy(w3_hbm.at[e], w3s, wsem.at[1]),
                pltpu.make_async_copy(w2_hbm.at[e], w2s, wsem.at[2]))

    @pl.when(w < n_ref[0])
    def _():
        @pl.when(w == 0)
        def _():
            for cp in fetch(e_ref[0]):
                cp.start()

        @pl.when(jnp.logical_or(w == 0, e_ref[w] != e_ref[prev]))
        def _():
            for cp in fetch(e_ref[w]):
                cp.wait()
            w1b[...] = w1s[...].astype(BF16)
            w3b[...] = w3s[...].astype(BF16)
            w2b[...] = w2s[...].astype(BF16)

            @pl.when(nxt_ref[w] >= 0)
            def _():
                for cp in fetch(nxt_ref[w]):
                    cp.start()

        d = xb.shape[1]
        tr = _token_rows(d)
        new_block = jnp.logical_or(w == 0, b_ref[w] != b_ref[prev])

        @pl.when(new_block)
        def _():
            for s in range(tr):
                lo, hi = _unpack_slab(xs_ref, s, EXPERT_ROWS, tr)
                xb[:, s * LANES:(s + 1) * LANES] = lo.astype(BF16)
                xb[:, d // 2 + s * LANES:d // 2 + (s + 1) * LANES] = hi.astype(BF16)

        x = xb[...]
        h1 = jnp.dot(x, w1b[...], preferred_element_type=F32)
        h3 = jnp.dot(x, w3b[...], preferred_element_type=F32)
        act = (_silu(h1) * h3).astype(BF16)
        _pack_rows(jnp.dot(act, w2b[...], preferred_element_type=F32), yp)
        prow = lax.broadcasted_iota(I32, (EXPERT_ROWS * tr, 1), 0)
        first = (lo_ref[w] - b_ref[w] * EXPERT_ROWS) * tr
        last = (hi_ref[w] - b_ref[w] * EXPERT_ROWS) * tr
        mine = jnp.logical_and(prow >= first, prow < last)

        @pl.when(new_block)
        def _():
            ys_ref[...] = jnp.where(mine, yp[...], jnp.zeros((), U32))

        @pl.when(jnp.logical_not(new_block))
        def _():
            ys_ref[...] = jnp.where(mine, yp[...], ys_ref[...])


def _experts(item_expert, item_block, item_lo, item_hi, item_next, n_items, xs, w1, w3, w2):
    _, d, de = w1.shape
    tr = _token_rows(d)
    rows = lambda w, e, b, lo, hi, nx, n: (b[w], 0)
    hbm = pl.BlockSpec(memory_space=pl.ANY)
    return pl.pallas_call(
        _experts_kernel,
        grid_spec=pltpu.PrefetchScalarGridSpec(
            num_scalar_prefetch=6,
            grid=(item_expert.shape[0],),
            in_specs=[pl.BlockSpec((EXPERT_ROWS * tr, LANES), rows), hbm, hbm, hbm],
            out_specs=pl.BlockSpec((EXPERT_ROWS * tr, LANES), rows),
            scratch_shapes=[pltpu.VMEM((d, de), F32), pltpu.VMEM((d, de), F32), pltpu.VMEM((de, d), F32),
                            pltpu.VMEM((d, de), BF16), pltpu.VMEM((d, de), BF16), pltpu.VMEM((de, d), BF16),
                            pltpu.VMEM((EXPERT_ROWS, d), BF16), pltpu.VMEM((EXPERT_ROWS * tr, LANES), U32),
                            pltpu.SemaphoreType.DMA((3,))],
        ),
        out_shape=jax.ShapeDtypeStruct(xs.shape, U32),
        compiler_params=_params("arbitrary"),
        name="experts",
    )(item_expert, item_block, item_lo, item_hi, item_next, n_items, xs, w1, w3, w2)


def _shared_kernel(h_ref, w1_ref, w3_ref, w2_ref, y_ref):
    x = h_ref[...].astype(BF16)
    h1 = jnp.dot(x, w1_ref[...], preferred_element_type=F32)
    h3 = jnp.dot(x, w3_ref[...], preferred_element_type=F32)
    y_ref[...] = jnp.dot((_silu(h1) * h3).astype(BF16), w2_ref[...], preferred_element_type=F32)


def _shared(h2, w1_b, w3_b, w2_b):
    t, d = h2.shape
    de = w1_b.shape[1]
    tm = 512
    row = pl.BlockSpec((tm, d), lambda i: (i, 0))
    return pl.pallas_call(
        _shared_kernel,
        grid=(t // tm,),
        in_specs=[row, pl.BlockSpec((d, de), lambda i: (0, 0)), pl.BlockSpec((d, de), lambda i: (0, 0)),
                  pl.BlockSpec((de, d), lambda i: (0, 0))],
        out_specs=row,
        out_shape=jax.ShapeDtypeStruct((t, d), F32),
        compiler_params=_params("arbitrary"),
        name="shared_expert",
    )(h2, w1_b, w3_b, w2_b)


def _combine_kernel(dest_hbm, ys_hbm, x1_ref, ysh_ref, wt_ref, gate_ref, fg_ref, o_ref,
                    rows, idx_smem, isem, gsem):
    i = pl.program_id(0)
    n = pl.num_programs(0)
    tm, d = x1_ref.shape
    tr = _token_rows(d)
    slot = i % 2
    idx_copy = functools.partial(_idx_copy, dest_hbm, idx_smem, isem)

    def issue_gathers(s):
        base = s * DEST_PER_TILE

        def body(c, carry):
            for u in range(ROW_DMA_UNROLL):
                t = c * ROW_DMA_UNROLL + u
                for k in range(TOP_K):
                    src = idx_smem[base + t * TOP_K + k]
                    pltpu.make_async_copy(_token_rows_at(ys_hbm, src, tr), _token_rows_at(rows.at[s, k], t, tr),
                                          gsem.at[s]).start(priority=k % 2)
            return carry

        lax.fori_loop(0, tm // ROW_DMA_UNROLL, body, 0)

    @pl.when(i == 0)
    def _():
        idx_copy(0, 0).start()
        idx_copy(0, 0).wait()
        issue_gathers(0)

        @pl.when(n > 1)
        def _():
            idx_copy(1, 1).start()

    @pl.when(i + 1 < n)
    def _():
        idx_copy(i + 1, 1 - slot).wait()
        issue_gathers(1 - slot)

    @pl.when(i + 2 < n)
    def _():
        idx_copy(i + 2, slot).start()

    for k in range(TOP_K):
        pltpu.make_async_copy(ys_hbm.at[pl.ds(0, tm * tr), :], rows.at[slot, k], gsem.at[slot]).wait()

    w = wt_ref[...]
    ssq = jnp.zeros((tm, 1), F32)
    for s in range(tr):
        cols = (slice(s * LANES, (s + 1) * LANES), slice(d // 2 + s * LANES, d // 2 + (s + 1) * LANES))
        acc = [ysh_ref[:, cs] for cs in cols]
        for k in range(TOP_K):
            halves = _unpack_slab(rows.at[slot, k], s, tm, tr)
            acc = [a + v * w[:, k:k + 1] for a, v in zip(acc, halves)]
        for cs, a in zip(cols, acc):
            x2 = x1_ref[:, cs] + gate_ref[:, cs] * a
            o_ref[:, cs] = x2
            ssq = ssq + jnp.sum(x2 * x2, axis=-1, keepdims=True)
    o_ref[...] = o_ref[...] * lax.rsqrt(ssq * (1.0 / d) + NORM_EPS) * fg_ref[...]


def _combine(dest, ys, x1, ysh, w_tok, gate2, final_g, seq):
    t, d = x1.shape
    tr = _token_rows(d)
    tm = DEST_TOKENS
    per_b = seq // tm
    row = pl.BlockSpec((tm, d), lambda i: (i, 0))
    return pl.pallas_call(
        _combine_kernel,
        grid=(t // tm,),
        in_specs=[pl.BlockSpec(memory_space=pl.ANY), pl.BlockSpec(memory_space=pl.ANY), row, row,
                  pl.BlockSpec((tm, TOP_K), lambda i: (i, 0)),
                  pl.BlockSpec((None, 1, d), lambda i: (i // per_b, 0, 0)),
                  pl.BlockSpec((1, d), lambda i: (0, 0))],
        out_specs=row,
        out_shape=jax.ShapeDtypeStruct((t, d), F32),
        scratch_shapes=[pltpu.VMEM((2, TOP_K, tm * tr, LANES), U32), pltpu.SMEM((2 * DEST_PER_TILE,), I32),
                        pltpu.SemaphoreType.DMA((2,)), pltpu.SemaphoreType.DMA((2,))],
        compiler_params=_params("arbitrary"),
        name="combine",
    )(dest, ys, x1, ysh, w_tok, gate2, final_g)


def _mixer(x2, mod6, cos_t, sin_t, bsz, seq, p):
    t, d = x2.shape
    shift1, scale1, gate1, shift2, scale2, _ = mod6
    d_rnn = p["conv_w"].shape[1]
    att_width = len(DILATION_GROUPS) * GROUP_COLS
    q_col = 2 * d_rnn
    gate_col = 2 * d_rnn + 3 * att_width
    w_in_b = p["w_in"].astype(BF16)

    h1 = _norm_mod(x2, p["norm1_g"].reshape(1, d), shift1, scale1, seq)
    rnn = _proj_act(h1, w_in_b, 0, 2 * d_rnn, d_rnn // GROUP_COLS, _gelu_tanh, "proj_rnn")
    gates = _proj_act(h1, w_in_b, gate_col, 2 * d, 0, jax.nn.sigmoid, "proj_gates")
    ya = _rglru(rnn, p["conv_w"], p["conv_b"].reshape(1, d_rnn),
                p["rg_wa"].astype(BF16), p["rg_ba"].reshape(1, d_rnn),
                p["rg_wi"].astype(BF16), p["rg_bi"].reshape(1, d_rnn),
                p["rg_lambda"].reshape(1, d_rnn), bsz, seq, d_rnn)

    outs, lses = [], []
    for g, (window, dilation) in enumerate(DILATION_GROUPS):
        qkv = _proj_qkv(h1, w_in_b, cos_t, sin_t, g, dilation, bsz, seq, q_col, att_width)
        o, l = _attention_group(qkv, g, window, dilation)
        outs.append(o)
        lses.append(l)

    merged = _merge(ya, outs, lses, gates, p["w_proj_rnn"].astype(BF16), p["w_proj_attn"].astype(BF16), seq)
    return _out_proj(merged, p["w_out"].astype(BF16), x2, gate1, p["norm2_g"].reshape(1, d),
                     shift2, scale2, seq)


def _moe(h2, h2p, p):
    t, d = h2.shape
    ne = p["router_w"].shape[1]
    idx_t, w_t, rank_t, cnt = _router(h2, p["router_w"].T.astype(BF16), p["router_bias"].reshape(ne, 1))

    counts = cnt[:, 0].astype(I32)
    ends = jnp.cumsum(counts).astype(I32)
    starts = ends - counts
    n_rows = t * TOP_K
    first_blk = starts // EXPERT_ROWS
    n_blk_e = jnp.where(counts > 0, (ends - 1) // EXPERT_ROWS - first_blk + 1, 0)
    item_end = jnp.cumsum(n_blk_e).astype(I32)
    item_start = item_end - n_blk_e
    n_items = item_end[-1]
    max_items = n_rows // EXPERT_ROWS + ne
    w = jnp.minimum(jnp.arange(max_items, dtype=I32), n_items - 1)
    item_expert = jnp.minimum(jnp.searchsorted(item_end, w, side="right"), ne - 1).astype(I32)
    item_block = first_blk[item_expert] + (w - item_start[item_expert])
    after = item_end[item_expert]
    item_next = jnp.where(after < n_items, item_expert[jnp.minimum(after, max_items - 1)], -1).astype(I32)

    dest = _dest_rows(starts, idx_t, rank_t)
    dest = jnp.transpose(dest, (0, 2, 1)).reshape(t // DEST_TOKENS, DEST_PER_TILE)
    xs = _dispatch(dest, h2p, _token_rows(d))
    ys = _experts(item_expert, item_block, starts[item_expert], ends[item_expert], item_next,
                  n_items.reshape(1), xs, p["exp_w1"], p["exp_w3"], p["exp_w2"])
    ysh = _shared(h2, p["sh_w1"].astype(BF16), p["sh_w3"].astype(BF16), p["sh_w2"].astype(BF16))
    return dest, ys, ysh, w_t.T


def kernel(x, c, positions, ada_w, ada_b, norm1_g, w_in, conv_w, conv_b, rg_wa, rg_ba, rg_wi, rg_bi, rg_lambda, w_proj_rnn, w_proj_attn, w_out, norm2_g, router_w, router_bias, exp_w1, exp_w3, exp_w2, sh_w1, sh_w3, sh_w2, final_g):
    bsz, seq, d = x.shape
    assert ada_w.shape[0] == 1, "the fused final norm assumes a single layer"
    t = bsz * seq
    x2 = x.reshape(t, d)
    first = lambda a: a.reshape(a.shape[1:])

    half = HEAD_DIM // 2
    inv_freq = ROPE_THETA ** (-jnp.arange(half, dtype=F32) * 2.0 / HEAD_DIM)
    freq = jnp.concatenate([inv_freq, inv_freq]).reshape(1, HEAD_DIM)
    sign = jnp.concatenate([-jnp.ones((half,), F32), jnp.ones((half,), F32)]).reshape(1, HEAD_DIM)
    cos_t, sin_t = _rope_tables(positions.reshape(t, 1), freq, sign)

    c_pad = jnp.zeros((SUBLANES, d), F32).at[:bsz].set(c)
    mod = _ada_mod(c_pad, first(ada_w), ada_b.reshape(1, -1))
    mod6 = tuple(mod[:bsz, k * d:(k + 1) * d].reshape(bsz, 1, d) for k in range(6))

    p = dict(norm1_g=first(norm1_g), w_in=first(w_in), conv_w=first(conv_w), conv_b=first(conv_b),
             rg_wa=first(rg_wa), rg_ba=first(rg_ba), rg_wi=first(rg_wi), rg_bi=first(rg_bi),
             rg_lambda=first(rg_lambda), w_proj_rnn=first(w_proj_rnn), w_proj_attn=first(w_proj_attn),
             w_out=first(w_out), norm2_g=first(norm2_g), router_w=first(router_w),
             router_bias=first(router_bias), exp_w1=first(exp_w1), exp_w3=first(exp_w3),
             exp_w2=first(exp_w2), sh_w1=first(sh_w1), sh_w3=first(sh_w3), sh_w2=first(sh_w2))
    x1, h2, h2p = _mixer(x2, mod6, cos_t, sin_t, bsz, seq, p)
    dest, ys, ysh, w_tok = _moe(h2, h2p, p)
    out = _combine(dest, ys, x1, ysh, w_tok, mod6[5], final_g.reshape(1, d), seq)
    return out.reshape(bsz, seq, d)
```

```python
import functools
import math

import jax
import jax.numpy as jnp
import numpy as np
from jax import lax
from jax.experimental import pallas as pl
from jax.experimental.pallas import tpu as pltpu

F32 = jnp.float32
BF16 = jnp.bfloat16
I32 = jnp.int32

HEAD_DIM = 128
HEADS_PER_GROUP = 4
DILATION_GROUPS = ((128, 1), (512, 4), (2048, 16))
ROPE_THETA = 10000.0
CONV_WIDTH = 4
LRU_C = 8.0
TOP_K = 8
N_EXPERT_GROUPS = 8
TOPK_GROUPS = 4
ROUTED_SCALE = 2.5
NORM_EPS = 1e-6

LANES = 128
SUBLANES = 8
VMEM_LIMIT_BYTES = 56 * 1024 * 1024

GROUP_COLS = HEADS_PER_GROUP * HEAD_DIM
EXPERT_ROWS = 256
DEST_TOKENS = 128


def _params(*sem):
    return pltpu.CompilerParams(dimension_semantics=sem, vmem_limit_bytes=VMEM_LIMIT_BYTES)


def _gelu_tanh(x):
    return 0.5 * x * (1.0 + jnp.tanh(math.sqrt(2.0 / math.pi) * (x + 0.044715 * (x * x * x))))


def _silu(x):
    return x * jax.nn.sigmoid(x)


def _rms(x, g):
    ms = jnp.mean(x * x, axis=-1, keepdims=True)
    return x * lax.rsqrt(ms + NORM_EPS) * g


def _ada_kernel(c_ref, w_ref, b_ref, o_ref):
    a = _silu(c_ref[...]).astype(BF16)
    o_ref[...] = jnp.dot(a, w_ref[...].astype(BF16), preferred_element_type=F32) + b_ref[...]


def _ada_mod(c_pad, ada_w, ada_b):
    rows, d = c_pad.shape
    n = ada_w.shape[1]
    tn = 1024
    return pl.pallas_call(
        _ada_kernel,
        grid=(n // tn,),
        in_specs=[
            pl.BlockSpec((rows, d), lambda j: (0, 0)),
            pl.BlockSpec((d, tn), lambda j: (0, j)),
            pl.BlockSpec((1, tn), lambda j: (0, j)),
        ],
        out_specs=pl.BlockSpec((rows, tn), lambda j: (0, j)),
        out_shape=jax.ShapeDtypeStruct((rows, n), F32),
        compiler_params=_params("arbitrary"),
        name="ada_mod",
    )(c_pad, ada_w, ada_b)


def _rope_kernel(pos_ref, freq_ref, sign_ref, cos_ref, sin_ref):
    ang = pos_ref[...].astype(F32) * freq_ref[...]
    cos_ref[...] = jnp.cos(ang)
    sin_ref[...] = jnp.sin(ang) * sign_ref[...]


def _rope_tables(pos_col, freq, sign):
    t = pos_col.shape[0]
    tm = 1024
    return pl.pallas_call(
        _rope_kernel,
        grid=(t // tm,),
        in_specs=[
            pl.BlockSpec((tm, 1), lambda i: (i, 0)),
            pl.BlockSpec((1, HEAD_DIM), lambda i: (0, 0)),
            pl.BlockSpec((1, HEAD_DIM), lambda i: (0, 0)),
        ],
        out_specs=[pl.BlockSpec((tm, HEAD_DIM), lambda i: (i, 0))] * 2,
        out_shape=[jax.ShapeDtypeStruct((t, HEAD_DIM), F32)] * 2,
        compiler_params=_params("arbitrary"),
        name="rope_tables",
    )(pos_col, freq, sign)


def _norm_kernel(x_ref, g_ref, sh_ref, sc_ref, h_ref):
    h_ref[...] = (_rms(x_ref[...], g_ref[...]) * (1.0 + sc_ref[...]) + sh_ref[...]).astype(h_ref.dtype)


def _norm_mod(x2, g, shift, scale, seq):
    t, d = x2.shape
    tm = 1024
    per_b = seq // tm
    bvec = pl.BlockSpec((None, 1, d), lambda i: (i // per_b, 0, 0))
    return pl.pallas_call(
        _norm_kernel,
        grid=(t // tm,),
        in_specs=[pl.BlockSpec((tm, d), lambda i: (i, 0)), pl.BlockSpec((1, d), lambda i: (0, 0)), bvec, bvec],
        out_specs=pl.BlockSpec((tm, d), lambda i: (i, 0)),
        out_shape=jax.ShapeDtypeStruct((t, d), BF16),
        compiler_params=_params("arbitrary"),
        name="norm_mod",
    )(x2, g, shift, scale)


def _proj_act_kernel(h_ref, w_ref, o_ref, *, act):
    acc = jnp.dot(h_ref[...], w_ref[...], preferred_element_type=F32)
    o_ref[...] = (acc if act is None else act(acc)).astype(o_ref.dtype)


def _sigmoid_tanh(x):
    return 0.5 + 0.5 * jnp.tanh(0.5 * x)


def _proj_act(h1, w_in_b, col0, ncols, act, name):
    t, d = h1.shape
    tm, tn = 1024, GROUP_COLS
    j0 = col0 // tn
    return pl.pallas_call(
        functools.partial(_proj_act_kernel, act=act),
        grid=(t // tm, ncols // tn),
        in_specs=[pl.BlockSpec((tm, d), lambda i, j: (i, 0)),
                  pl.BlockSpec((d, tn), lambda i, j: (0, j0 + j))],
        out_specs=pl.BlockSpec((tm, tn), lambda i, j: (i, j)),
        out_shape=jax.ShapeDtypeStruct((t, ncols), BF16),
        compiler_params=_params("arbitrary", "arbitrary"),
        name=name,
    )(h1, w_in_b)


def _proj_qkv_kernel(h_ref, w_ref, cos_ref, sin_ref, o_ref, scr, *, dilation):
    tm = h_ref.shape[0]
    acc = jnp.dot(h_ref[...], w_ref[...], preferred_element_type=F32)
    j = pl.program_id(1)

    @pl.when(j < 2)
    def _():
        scale = jnp.where(j == 0, HEAD_DIM ** -0.5, 1.0).astype(F32)
        c = cos_ref[...] * scale
        s = sin_ref[...] * scale
        for h in range(HEADS_PER_GROUP):
            v = acc[:, h * HEAD_DIM:(h + 1) * HEAD_DIM]
            scr[h] = v * c + pltpu.roll(v, HEAD_DIM // 2, axis=1) * s

    @pl.when(j == 2)
    def _():
        for h in range(HEADS_PER_GROUP):
            scr[h] = acc[:, h * HEAD_DIM:(h + 1) * HEAD_DIM]

    sub = tm // dilation
    for r in range(dilation):
        for h in range(HEADS_PER_GROUP):
            rows = scr[h] if dilation == 1 else scr[h, pl.ds(r, sub, stride=dilation), :]
            o_ref[r, :, h * HEAD_DIM:(h + 1) * HEAD_DIM] = rows.astype(o_ref.dtype)


def _proj_qkv(h1, w_in_b, cos_t, sin_t, g, dilation, bsz, seq, q_col, att_width):
    t, d = h1.shape
    tm, tn = 1024, GROUP_COLS
    per_b = seq // tm
    j0 = q_col // tn + g
    step = att_width // tn
    sub = tm // dilation
    return pl.pallas_call(
        functools.partial(_proj_qkv_kernel, dilation=dilation),
        grid=(t // tm, 3),
        in_specs=[pl.BlockSpec((tm, d), lambda i, j: (i, 0)),
                  pl.BlockSpec((d, tn), lambda i, j: (0, j0 + step * j)),
                  pl.BlockSpec((tm, HEAD_DIM), lambda i, j: (i, 0)),
                  pl.BlockSpec((tm, HEAD_DIM), lambda i, j: (i, 0))],
        out_specs=pl.BlockSpec((None, dilation, sub, tn), lambda i, j: (i // per_b, 0, i % per_b, j)),
        out_shape=jax.ShapeDtypeStruct((bsz, dilation, seq // dilation, 3 * tn), BF16),
        scratch_shapes=[pltpu.VMEM((HEADS_PER_GROUP, tm, HEAD_DIM), F32)],
        compiler_params=_params("arbitrary", "arbitrary"),
        name=f"proj_qkv_g{g}",
    )(h1, w_in_b, cos_t, sin_t)


def _rglru_kernel(xr_ref, gr_ref, cw_ref, cb_ref, wa_ref, ba_ref, wi_ref, bi_ref, lam_ref,
                  ya_ref, xbuf, a_scr, b_scr, hcar):
    tt = xr_ref.shape[0]
    halo = SUBLANES

    @pl.when(pl.program_id(2) == 0)
    def _():
        xbuf[0:halo, :] = jnp.zeros((halo, LANES), F32)
        hcar[...] = jnp.zeros_like(hcar)

    xbuf[halo:halo + tt, :] = xr_ref[...].astype(F32)
    u = cb_ref[...] + cw_ref[CONV_WIDTH - 1:CONV_WIDTH, :] * xbuf[halo:halo + tt, :]
    for j in range(CONV_WIDTH - 1):
        s = CONV_WIDTH - 1 - j
        u = u + cw_ref[j:j + 1, :] * xbuf[halo - s:halo - s + tt, :]
    xbuf[0:halo, :] = xbuf[tt:tt + halo, :]

    ub = u.astype(BF16)
    r = 0.5 + 0.5 * jnp.tanh(0.5 * (jnp.dot(ub, wa_ref[...], preferred_element_type=F32) + ba_ref[...]))
    ig = 0.5 + 0.5 * jnp.tanh(0.5 * (jnp.dot(ub, wi_ref[...], preferred_element_type=F32) + bi_ref[...]))
    z = -lam_ref[...]
    softplus = jnp.maximum(z, 0.0) + jnp.log1p(jnp.exp(-jnp.abs(z)))
    t = jnp.tanh((-0.5 * LRU_C) * r * softplus)
    q = 1.0 / (1.0 - t)
    a = (1.0 + t) * q
    b = (2.0 * q) * jnp.sqrt(-t) * (ig * u)

    seg = tt // SUBLANES
    pitch = a_scr.shape[0] // SUBLANES
    for s in range(SUBLANES):
        a_scr[s * pitch:s * pitch + seg, :] = a[s * seg:(s + 1) * seg, :]
        b_scr[s * pitch:s * pitch + seg, :] = b[s * seg:(s + 1) * seg, :]
    h = jnp.zeros((SUBLANES, LANES), F32)
    prod = jnp.ones((SUBLANES, LANES), F32)
    for j in range(seg):
        rows_j = pl.ds(j, SUBLANES, stride=pitch)
        aj = a_scr[rows_j, :]
        h = aj * h + b_scr[rows_j, :]
        prod = aj * prod
        b_scr[rows_j, :] = h
        a_scr[rows_j, :] = prod
    c = hcar[0:1, :]
    entering = []
    for s in range(SUBLANES):
        entering.append(c)
        c = h[s:s + 1, :] + prod[s:s + 1, :] * c
    hcar[...] = jnp.broadcast_to(c, hcar.shape)
    start = jnp.concatenate(entering, axis=0)
    for j in range(seg):
        rows_j = pl.ds(j, SUBLANES, stride=pitch)
        b_scr[rows_j, :] = b_scr[rows_j, :] + a_scr[rows_j, :] * start
    for s in range(SUBLANES):
        rs = slice(s * seg, (s + 1) * seg)
        ya_ref[rs, :] = (b_scr[s * pitch:s * pitch + seg, :] * gr_ref[rs, :].astype(F32)).astype(ya_ref.dtype)


def _rglru(xr, gr, conv_w, conv_b, wa_b, ba, wi_b, bi, lam, bsz, seq, d_rnn):
    t = xr.shape[0]
    tt = 512
    nct = d_rnn // LANES
    per_b = seq // tt
    row = lambda b, c, s: (b * per_b + s, c)
    vec = lambda b, c, s: (0, c)
    return pl.pallas_call(
        _rglru_kernel,
        grid=(bsz, nct, per_b),
        in_specs=[
            pl.BlockSpec((tt, LANES), row),
            pl.BlockSpec((tt, LANES), row),
            pl.BlockSpec((CONV_WIDTH, LANES), vec),
            pl.BlockSpec((1, LANES), vec),
            pl.BlockSpec((None, LANES, LANES), lambda b, c, s: (c, 0, 0)),
            pl.BlockSpec((1, LANES), vec),
            pl.BlockSpec((None, LANES, LANES), lambda b, c, s: (c, 0, 0)),
            pl.BlockSpec((1, LANES), vec),
            pl.BlockSpec((1, LANES), vec),
        ],
        out_specs=pl.BlockSpec((tt, LANES), row),
        out_shape=jax.ShapeDtypeStruct((t, d_rnn), BF16),
        scratch_shapes=[pltpu.VMEM((tt + SUBLANES, LANES), F32), pltpu.VMEM((tt + SUBLANES * SUBLANES, LANES), F32),
                        pltpu.VMEM((tt + SUBLANES * SUBLANES, LANES), F32), pltpu.VMEM((SUBLANES, LANES), F32)],
        compiler_params=_params("arbitrary", "arbitrary", "arbitrary"),
        name="rglru",
    )(xr, gr, conv_w, conv_b, wa_b, ba, wi_b, bi, lam)


ATTN_BLOCKS = 4


def _attn_kernel(q_ref, kc_ref, kp_ref, vc_ref, vp_ref, o_ref, l_ref, *, blk):
    nq = q_ref.shape[0] // blk
    not_first = pl.program_id(2) > 0
    qi = lax.broadcasted_iota(I32, (blk, blk), 0)
    kj = lax.broadcasted_iota(I32, (blk, blk), 1)
    tri_prev = kj >= qi
    mask_cur = kj <= qi
    nt = (((1,), (1,)), ((), ()))
    units = [(j, h) for j in range(nq) for h in range(HEADS_PER_GROUP)]
    rows = lambda j: slice(j * blk, (j + 1) * blk)
    cols = lambda h: slice(h * HEAD_DIM, (h + 1) * HEAD_DIM)

    def prev_kv(ref, pref, j, h):
        return pref[:, cols(h)] if j == 0 else ref[rows(j - 1), cols(h)]

    scores = []
    for j, h in units:
        q = q_ref[rows(j), cols(h)]
        sp = lax.dot_general(q, prev_kv(kc_ref, kp_ref, j, h), nt, preferred_element_type=F32)
        sc = lax.dot_general(q, kc_ref[rows(j), cols(h)], nt, preferred_element_type=F32)
        mask_prev = jnp.logical_and(tri_prev, not_first) if j == 0 else tri_prev
        scores.append((jnp.where(mask_prev, sp, -jnp.inf), jnp.where(mask_cur, sc, -jnp.inf)))
    maxes = [jnp.maximum(jnp.max(sp, axis=-1, keepdims=True), jnp.max(sc, axis=-1, keepdims=True))
             for sp, sc in scores]
    probs = [(jnp.exp(sp - m), jnp.exp(sc - m)) for (sp, sc), m in zip(scores, maxes)]
    dens = [jnp.sum(pp, axis=-1, keepdims=True) + jnp.sum(pc, axis=-1, keepdims=True) for pp, pc in probs]
    for (j, h), (pp, pc), m, den in zip(units, probs, maxes, dens):
        inv = 1.0 / den
        out = (jnp.dot((pp * inv).astype(BF16), prev_kv(vc_ref, vp_ref, j, h), preferred_element_type=F32)
               + jnp.dot((pc * inv).astype(BF16), vc_ref[rows(j), cols(h)], preferred_element_type=F32))
        o_ref[rows(j), cols(h)] = out.astype(o_ref.dtype)
        l_ref[rows(j), cols(h)] = jnp.broadcast_to(m + jnp.log(den), (blk, HEAD_DIM))


def _attention_group(qkv, g, window, dilation):
    bsz, _, length, _ = qkv.shape
    blk = window // dilation
    nb = length // blk
    nq = math.gcd(ATTN_BLOCKS, nb)
    cur = lambda c: (lambda b, r, n: (b, r, n, c))
    prev = lambda c: (lambda b, r, n: (b, r, jnp.maximum(n * nq - 1, 0), c))
    spec = lambda f: pl.BlockSpec((None, None, nq * blk, GROUP_COLS), f)
    pspec = lambda f: pl.BlockSpec((None, None, blk, GROUP_COLS), f)
    out_sds = lambda dt: jax.ShapeDtypeStruct((bsz, dilation, length, GROUP_COLS), dt)
    return pl.pallas_call(
        functools.partial(_attn_kernel, blk=blk),
        grid=(bsz, dilation, nb // nq),
        in_specs=[spec(cur(0)), spec(cur(1)), pspec(prev(1)), spec(cur(2)), pspec(prev(2))],
        out_specs=[spec(cur(0)), spec(cur(0))],
        out_shape=[out_sds(BF16), out_sds(F32)],
        compiler_params=_params("arbitrary", "arbitrary", "arbitrary"),
        name=f"attn_g{g}",
    )(qkv, qkv, qkv, qkv, qkv)


def _merge_kernel(ya_ref, o0, o1, o2, l0, l1, l2, ga_ref, gb_ref, wr_ref, wa_ref, m_ref,
                  yb_scr, o_scr, l_scr):
    tm = ya_ref.shape[0]

    @pl.when(pl.program_id(1) == 0)
    def _():
        for g, (o_ref, l_ref) in enumerate(((o0, l0), (o1, l1), (o2, l2))):
            dil = o_ref.shape[0]
            for r in range(dil):
                for h in range(HEADS_PER_GROUP):
                    cs = slice(h * HEAD_DIM, (h + 1) * HEAD_DIM)
                    if dil == 1:
                        o_scr[g, h] = o_ref[r, :, cs].astype(F32)
                        l_scr[g, h] = l_ref[r, :, cs]
                    else:
                        o_scr[g, h, pl.ds(r, tm // dil, stride=dil), :] = o_ref[r, :, cs].astype(F32)
                        l_scr[g, h, pl.ds(r, tm // dil, stride=dil), :] = l_ref[r, :, cs]
        for h in range(HEADS_PER_GROUP):
            la, lb, lc = l_scr[0, h], l_scr[1, h], l_scr[2, h]
            m = jnp.maximum(jnp.maximum(la, lb), lc)
            ea, eb, ec = jnp.exp(la - m), jnp.exp(lb - m), jnp.exp(lc - m)
            tot = ea + eb + ec
            yb = (ea / tot) * o_scr[0, h] + (eb / tot) * o_scr[1, h] + (ec / tot) * o_scr[2, h]
            yb_scr[:, h * HEAD_DIM:(h + 1) * HEAD_DIM] = yb.astype(BF16)

    pa = jnp.dot(ya_ref[...], wr_ref[...], preferred_element_type=F32)
    pb = jnp.dot(yb_scr[...], wa_ref[...], preferred_element_type=F32)
    m_ref[...] = (ga_ref[...].astype(F32) * pa + gb_ref[...].astype(F32) * pb).astype(m_ref.dtype)


def _merge(ya, outs, lses, gates, wr_b, wa_b, seq):
    t, d_rnn = ya.shape
    d = wr_b.shape[1]
    tm, tn = 512, GROUP_COLS
    per_b = seq // tm
    n_groups = len(outs)

    def grp(o):
        dil = o.shape[1]
        return pl.BlockSpec((None, dil, tm // dil, GROUP_COLS), lambda i, j: (i // per_b, 0, i % per_b, 0))

    return pl.pallas_call(
        _merge_kernel,
        grid=(t // tm, d // tn),
        in_specs=[
            pl.BlockSpec((tm, d_rnn), lambda i, j: (i, 0)),
            *[grp(o) for o in outs], *[grp(l) for l in lses],
            pl.BlockSpec((tm, tn), lambda i, j: (i, j)),
            pl.BlockSpec((tm, tn), lambda i, j: (i, d // tn + j)),
            pl.BlockSpec((d_rnn, tn), lambda i, j: (0, j)),
            pl.BlockSpec((GROUP_COLS, tn), lambda i, j: (0, j)),
        ],
        out_specs=pl.BlockSpec((tm, tn), lambda i, j: (i, j)),
        out_shape=jax.ShapeDtypeStruct((t, d), BF16),
        scratch_shapes=[pltpu.VMEM((tm, GROUP_COLS), BF16),
                        pltpu.VMEM((n_groups, HEADS_PER_GROUP, tm, HEAD_DIM), F32),
                        pltpu.VMEM((n_groups, HEADS_PER_GROUP, tm, HEAD_DIM), F32)],
        compiler_params=_params("arbitrary", "arbitrary"),
        name="merge_proj",
    )(ya, *outs, *lses, gates, gates, wr_b, wa_b)


U32 = jnp.uint32
HIGH_HALF = np.uint32(0xFFFF0000)


def _token_rows(d):
    assert d % (2 * LANES) == 0
    return d // (2 * LANES)


def _pack_rows(v, dst_ref):
    rows, d = v.shape
    tr = _token_rows(d)
    lo = pltpu.bitcast(v[:, :d // 2].astype(BF16).astype(F32), U32)
    hi = pltpu.bitcast(v[:, d // 2:].astype(BF16).astype(F32), U32)
    word = (lo >> 16) | (hi & HIGH_HALF)
    for s in range(tr):
        dst_ref[pl.ds(s, rows, stride=tr), :] = word[:, s * LANES:(s + 1) * LANES]


def _unpack_slab(src_ref, s, rows, tr):
    word = src_ref[pl.ds(s, rows, stride=tr), :]
    return pltpu.bitcast(word << 16, F32), pltpu.bitcast(word & HIGH_HALF, F32)


def _outproj_kernel(m_ref, w_ref, x_ref, gate_ref, g2_ref, sh_ref, sc_ref, x1_ref, h2_ref, h2p_ref):
    x1 = x_ref[...] + gate_ref[...] * jnp.dot(m_ref[...], w_ref[...], preferred_element_type=F32)
    x1_ref[...] = x1
    h2 = _rms(x1, g2_ref[...]) * (1.0 + sc_ref[...]) + sh_ref[...]
    h2_ref[...] = h2.astype(h2_ref.dtype)
    _pack_rows(h2, h2p_ref)


def _out_proj(merged, w_out_b, x2, gate1, g2, shift2, scale2, seq):
    t, d = x2.shape
    tr = _token_rows(d)
    tm = 512
    per_b = seq // tm
    row = pl.BlockSpec((tm, d), lambda i: (i, 0))
    bvec = pl.BlockSpec((None, 1, d), lambda i: (i // per_b, 0, 0))
    return pl.pallas_call(
        _outproj_kernel,
        grid=(t // tm,),
        in_specs=[row, pl.BlockSpec((d, d), lambda i: (0, 0)), row, bvec,
                  pl.BlockSpec((1, d), lambda i: (0, 0)), bvec, bvec],
        out_specs=[row, row, pl.BlockSpec((tm * tr, LANES), lambda i: (i, 0))],
        out_shape=[jax.ShapeDtypeStruct((t, d), F32), jax.ShapeDtypeStruct((t, d), BF16),
                   jax.ShapeDtypeStruct((t * tr, LANES), U32)],
        compiler_params=_params("arbitrary"),
        name="out_proj",
    )(merged, w_out_b, x2, gate1, g2, shift2, scale2)


def _router_kernel(h_ref, rw_ref, bias_ref, idx_ref, w_ref, rank_ref, cnt_ref, carry):
    ne = rw_ref.shape[0]
    tm = h_ref.shape[0]
    gsz = ne // N_EXPERT_GROUPS

    @pl.when(pl.program_id(0) == 0)
    def _():
        carry[...] = jnp.zeros_like(carry)

    logits = lax.dot_general(rw_ref[...], h_ref[...].astype(BF16), (((1,), (1,)), ((), ())),
                             preferred_element_type=F32)
    scores = jax.nn.sigmoid(logits)
    sel = scores + bias_ref[...]
    row = lax.broadcasted_iota(I32, (ne, tm), 0)
    neg = -jnp.inf

    gscore = []
    rg = lax.broadcasted_iota(I32, (gsz, tm), 0)
    for g in range(N_EXPERT_GROUPS):
        sg = sel[g * gsz:(g + 1) * gsz, :]
        m1 = jnp.max(sg, axis=0, keepdims=True)
        i1 = jnp.min(jnp.where(sg == m1, rg, ne), axis=0, keepdims=True)
        m2 = jnp.max(jnp.where(rg == i1, neg, sg), axis=0, keepdims=True)
        gscore.append(m1 + m2)
    keep_rows = []
    for g in range(N_EXPERT_GROUPS):
        beaten = jnp.zeros((1, tm), I32)
        for o in range(N_EXPERT_GROUPS):
            if o == g:
                continue
            wins = (gscore[o] >= gscore[g]) if o < g else (gscore[o] > gscore[g])
            beaten = beaten + wins.astype(I32)
        keep_rows.append(jnp.broadcast_to(beaten, (gsz, tm)))
    cur = jnp.where(jnp.concatenate(keep_rows, axis=0) < TOPK_GROUPS, sel, neg)

    chosen = jnp.zeros((ne, tm), F32)
    picks, wts = [], []
    for _ in range(TOP_K):
        m = jnp.max(cur, axis=0, keepdims=True)
        ik = jnp.min(jnp.where(cur == m, row, ne), axis=0, keepdims=True)
        hit = row == ik
        wts.append(jnp.sum(jnp.where(hit, scores, 0.0), axis=0, keepdims=True))
        cur = jnp.where(hit, neg, cur)
        chosen = jnp.where(hit, 1.0, chosen)
        picks.append(ik)
    wsum = wts[0]
    for k in range(1, TOP_K):
        wsum = wsum + wts[k]

    ti = lax.broadcasted_iota(I32, (tm, tm), 0)
    tj = lax.broadcasted_iota(I32, (tm, tm), 1)
    upper = (ti < tj).astype(BF16)
    chosen_b = chosen.astype(BF16)
    before = jnp.dot(chosen_b, upper, preferred_element_type=F32)
    total = jnp.dot(chosen_b, jnp.ones((tm, LANES), BF16), preferred_element_type=F32)
    base = carry[...]
    pos = before + jnp.concatenate([base] * (tm // LANES), axis=1)
    for k in range(TOP_K):
        hit = row == picks[k]
        idx_ref[k:k + 1, :] = picks[k]
        w_ref[k:k + 1, :] = wts[k] / wsum * ROUTED_SCALE
        rank_ref[k:k + 1, :] = jnp.sum(jnp.where(hit, pos, 0.0), axis=0, keepdims=True).astype(I32)
    carry[...] = base + total
    cnt_ref[...] = base + total


def _router(h2, rw_t, bias_col):
    t, d = h2.shape
    ne = rw_t.shape[0]
    tm = 256
    kt = pl.BlockSpec((TOP_K, tm), lambda i: (0, i))
    return pl.pallas_call(
        _router_kernel,
        grid=(t // tm,),
        in_specs=[pl.BlockSpec((tm, d), lambda i: (i, 0)),
                  pl.BlockSpec((ne, d), lambda i: (0, 0)),
                  pl.BlockSpec((ne, 1), lambda i: (0, 0))],
        out_specs=[kt, kt, kt, pl.BlockSpec((ne, LANES), lambda i: (0, 0))],
        out_shape=[jax.ShapeDtypeStruct((TOP_K, t), I32), jax.ShapeDtypeStruct((TOP_K, t), F32),
                   jax.ShapeDtypeStruct((TOP_K, t), I32), jax.ShapeDtypeStruct((ne, LANES), F32)],
        scratch_shapes=[pltpu.VMEM((ne, LANES), F32)],
        compiler_params=_params("arbitrary"),
        name="router",
    )(h2, rw_t, bias_col)


def _dest_kernel(start_ref, idx_ref, rank_ref, dest_ref):
    ne = start_ref.shape[0]
    idx = idx_ref[...]

    def body(e, acc):
        return jnp.where(idx == e, start_ref[e], acc)

    dest = rank_ref[...] + lax.fori_loop(0, ne, body, jnp.zeros(idx.shape, I32))
    for j in range(dest_ref.shape[0]):
        dest_ref[j] = dest[:, j * DEST_TOKENS:(j + 1) * DEST_TOKENS]


def _dest_rows(starts, idx_t, rank_t):
    t = idx_t.shape[1]
    tb = 2048
    per = tb // DEST_TOKENS
    return pl.pallas_call(
        _dest_kernel,
        grid_spec=pltpu.PrefetchScalarGridSpec(
            num_scalar_prefetch=1,
            grid=(t // tb,),
            in_specs=[pl.BlockSpec((TOP_K, tb), lambda i, s: (0, i)),
                      pl.BlockSpec((TOP_K, tb), lambda i, s: (0, i))],
            out_specs=pl.BlockSpec((per, TOP_K, DEST_TOKENS), lambda i, s: (i, 0, 0)),
        ),
        out_shape=jax.ShapeDtypeStruct((t // DEST_TOKENS, TOP_K, DEST_TOKENS), I32),
        compiler_params=_params("arbitrary"),
        name="dest_rows",
    )(starts, idx_t, rank_t)


ROW_DMA_UNROLL = 4
DEST_PER_TILE = DEST_TOKENS * TOP_K


def _idx_copy(dest_hbm, idx_smem, isem, tile, s):
    return pltpu.make_async_copy(
        dest_hbm.at[tile], idx_smem.at[pl.ds(pl.multiple_of(s * DEST_PER_TILE, DEST_PER_TILE), DEST_PER_TILE)],
        isem.at[s])


def _token_rows_at(ref, token, tr):
    return ref.at[pl.ds(pl.multiple_of(token * tr, tr), tr), :]


def _dispatch_kernel(dest_hbm, h_ref, xs_hbm, idx_smem, isem, dsem, *, tr):
    i = pl.program_id(0)
    n = pl.num_programs(0)
    tm = h_ref.shape[0] // tr
    slot = i % 2
    idx_copy = functools.partial(_idx_copy, dest_hbm, idx_smem, isem)

    @pl.when(i == 0)
    def _():
        idx_copy(0, 0).start()

    idx_copy(i, slot).wait()

    @pl.when(i + 1 < n)
    def _():
        idx_copy(i + 1, 1 - slot).start()

    base = slot * DEST_PER_TILE

    def body(c, carry):
        for u in range(ROW_DMA_UNROLL):
            t = c * ROW_DMA_UNROLL + u
            for k in range(TOP_K):
                d = idx_smem[base + t * TOP_K + k]
                pltpu.make_async_copy(_token_rows_at(h_ref, t, tr), _token_rows_at(xs_hbm, d, tr),
                                      dsem).start(priority=k % 2)
        return carry

    lax.fori_loop(0, tm // ROW_DMA_UNROLL, body, 0)
    for k in range(TOP_K):
        pltpu.make_async_copy(h_ref, xs_hbm.at[pl.ds(0, tm * tr), :], dsem).wait()


def _dispatch(dest, h2p, tr):
    t = h2p.shape[0] // tr
    tm = DEST_TOKENS
    return pl.pallas_call(
        functools.partial(_dispatch_kernel, tr=tr),
        grid=(t // tm,),
        in_specs=[pl.BlockSpec(memory_space=pl.ANY), pl.BlockSpec((tm * tr, LANES), lambda i: (i, 0))],
        out_specs=pl.BlockSpec(memory_space=pl.ANY),
        out_shape=jax.ShapeDtypeStruct((t * TOP_K * tr, LANES), U32),
        scratch_shapes=[pltpu.SMEM((2 * DEST_PER_TILE,), I32), pltpu.SemaphoreType.DMA((2,)),
                        pltpu.SemaphoreType.DMA(())],
        compiler_params=_params("arbitrary"),
        name="dispatch",
    )(dest, h2p)


def _experts_kernel(e_ref, b_ref, lo_ref, hi_ref, nxt_ref, half_ref, n_ref, xs_ref, w1_hbm, w3_hbm, w2_hbm, ys_ref,
                    w1s, w3s, w2s, w1b, w3b, w2b, xb, yp, wsem):
    w = pl.program_id(0)
    prev = jnp.maximum(w - 1, 0)

    def fetch(e):
        return (pltpu.make_async_copy(w1_hbm.at[e], w1s, wsem.at[0]),
                pltpu.make_async_copy(w3_hbm.at[e], w3s, wsem.at[1]),
                pltpu.make_async_copy(w2_hbm.at[e], w2s, wsem.at[2]))

    @pl.when(w < n_ref[0])
    def _():
        @pl.when(w == 0)
        def _():
            for cp in fetch(e_ref[0]):
                cp.start()

        @pl.when(jnp.logical_or(w == 0, e_ref[w] != e_ref[prev]))
        def _():
            for cp in fetch(e_ref[w]):
                cp.wait()
            w1b[...] = w1s[...].astype(BF16)
            w3b[...] = w3s[...].astype(BF16)
            w2b[...] = w2s[...].astype(BF16)

            @pl.when(nxt_ref[w] >= 0)
            def _():
                for cp in fetch(nxt_ref[w]):
                    cp.start()

        d = xb.shape[1]
        tr = _token_rows(d)
        new_block = jnp.logical_or(w == 0, b_ref[w] != b_ref[prev])

        @pl.when(new_block)
        def _():
            for s in range(tr):
                lo, hi = _unpack_slab(xs_ref, s, EXPERT_ROWS, tr)
                xb[:, s * LANES:(s + 1) * LANES] = lo.astype(BF16)
                xb[:, d // 2 + s * LANES:d // 2 + (s + 1) * LANES] = hi.astype(BF16)
            ys_ref[...] = jnp.zeros(ys_ref.shape, U32)

        first = (lo_ref[w] - b_ref[w] * EXPERT_ROWS) * tr
        last = (hi_ref[w] - b_ref[w] * EXPERT_ROWS) * tr

        def run(row0, nrows):
            x = xb[pl.ds(row0, nrows), :]
            h1 = jnp.dot(x, w1b[...], preferred_element_type=F32)
            h3 = jnp.dot(x, w3b[...], preferred_element_type=F32)
            act = (_silu(h1) * h3).astype(BF16)
            span = pl.ds(row0 * tr, nrows * tr)
            _pack_rows(jnp.dot(act, w2b[...], preferred_element_type=F32), yp.at[span, :])
            prow = row0 * tr + lax.broadcasted_iota(I32, (nrows * tr, 1), 0)
            mine = jnp.logical_and(prow >= first, prow < last)
            ys_ref[span, :] = jnp.where(mine, yp[span, :], ys_ref[span, :])

        half_rows = EXPERT_ROWS // 2

        @pl.when(half_ref[w] == 0)
        def _():
            run(0, EXPERT_ROWS)

        @pl.when(half_ref[w] != 0)
        def _():
            run(pl.multiple_of((half_ref[w] - 1) * half_rows, half_rows), half_rows)


def _experts(item_expert, item_block, item_lo, item_hi, item_next, item_half, n_items, xs, w1, w3, w2):
    _, d, de = w1.shape
    tr = _token_rows(d)
    rows = lambda w, e, b, lo, hi, nx, hf, n: (b[w], 0)
    hbm = pl.BlockSpec(memory_space=pl.ANY)
    return pl.pallas_call(
        _experts_kernel,
        grid_spec=pltpu.PrefetchScalarGridSpec(
            num_scalar_prefetch=7,
            grid=(item_expert.shape[0],),
            in_specs=[pl.BlockSpec((EXPERT_ROWS * tr, LANES), rows), hbm, hbm, hbm],
            out_specs=pl.BlockSpec((EXPERT_ROWS * tr, LANES), rows),
            scratch_shapes=[pltpu.VMEM((d, de), F32), pltpu.VMEM((d, de), F32), pltpu.VMEM((de, d), F32),
                            pltpu.VMEM((d, de), BF16), pltpu.VMEM((d, de), BF16), pltpu.VMEM((de, d), BF16),
                            pltpu.VMEM((EXPERT_ROWS, d), BF16), pltpu.VMEM((EXPERT_ROWS * tr, LANES), U32),
                            pltpu.SemaphoreType.DMA((3,))],
        ),
        out_shape=jax.ShapeDtypeStruct(xs.shape, U32),
        compiler_params=_params("arbitrary"),
        name="experts",
    )(item_expert, item_block, item_lo, item_hi, item_next, item_half, n_items, xs, w1, w3, w2)


def _shared_kernel(h_ref, w1_ref, w3_ref, w2_ref, y_ref):
    x = h_ref[...].astype(BF16)
    h1 = jnp.dot(x, w1_ref[...], preferred_element_type=F32)
    h3 = jnp.dot(x, w3_ref[...], preferred_element_type=F32)
    y_ref[...] = jnp.dot((_silu(h1) * h3).astype(BF16), w2_ref[...], preferred_element_type=F32)


def _shared(h2, w1_b, w3_b, w2_b):
    t, d = h2.shape
    de = w1_b.shape[1]
    tm = 512
    row = pl.BlockSpec((tm, d), lambda i: (i, 0))
    return pl.pallas_call(
        _shared_kernel,
        grid=(t // tm,),
        in_specs=[row, pl.BlockSpec((d, de), lambda i: (0, 0)), pl.BlockSpec((d, de), lambda i: (0, 0)),
                  pl.BlockSpec((de, d), lambda i: (0, 0))],
        out_specs=row,
        out_shape=jax.ShapeDtypeStruct((t, d), F32),
        compiler_params=_params("arbitrary"),
        name="shared_expert",
    )(h2, w1_b, w3_b, w2_b)


def _combine_kernel(dest_hbm, ys_hbm, x1_ref, ysh_ref, wt_ref, gate_ref, fg_ref, o_ref,
                    rows, idx_smem, isem, gsem):
    i = pl.program_id(0)
    n = pl.num_programs(0)
    tm, d = x1_ref.shape
    tr = _token_rows(d)
    slot = i % 2
    idx_copy = functools.partial(_idx_copy, dest_hbm, idx_smem, isem)

    def issue_gathers(s):
        base = s * DEST_PER_TILE

        def body(c, carry):
            for u in range(ROW_DMA_UNROLL):
                t = c * ROW_DMA_UNROLL + u
                for k in range(TOP_K):
                    src = idx_smem[base + t * TOP_K + k]
                    pltpu.make_async_copy(_token_rows_at(ys_hbm, src, tr), _token_rows_at(rows.at[s, k], t, tr),
                                          gsem.at[s]).start(priority=k % 2)
            return carry

        lax.fori_loop(0, tm // ROW_DMA_UNROLL, body, 0)

    @pl.when(i == 0)
    def _():
        idx_copy(0, 0).start()
        idx_copy(0, 0).wait()
        issue_gathers(0)

        @pl.when(n > 1)
        def _():
            idx_copy(1, 1).start()

    nxt = 1 - slot

    @pl.when(i + 1 < n)
    def _():
        idx_copy(i + 1, nxt).wait()

    @pl.when(i + 2 < n)
    def _():
        idx_copy(i + 2, slot).start()

    def wait_gathers(s):
        for k in range(TOP_K):
            pltpu.make_async_copy(ys_hbm.at[pl.ds(0, tm * tr), :], rows.at[s, k], gsem.at[s]).wait()

    wait_gathers(slot)

    def issue_next(t0, t1):
        for t in range(t0, t1):
            for k in range(TOP_K):
                src = idx_smem[nxt * DEST_PER_TILE + t * TOP_K + k]
                pltpu.make_async_copy(_token_rows_at(ys_hbm, src, tr), rows.at[nxt, k, pl.ds(t * tr, tr), :],
                                      gsem.at[nxt]).start(priority=k % 2)

    w = wt_ref[...]
    ssq = jnp.zeros((tm, 1), F32)
    for s in range(tr):
        issue_next(s * tm // tr, (s + 1) * tm // tr)
        cols = (slice(s * LANES, (s + 1) * LANES), slice(d // 2 + s * LANES, d // 2 + (s + 1) * LANES))
        acc = [ysh_ref[:, cs] for cs in cols]
        for k in range(TOP_K):
            halves = _unpack_slab(rows.at[slot, k], s, tm, tr)
            acc = [a + v * w[:, k:k + 1] for a, v in zip(acc, halves)]
        for cs, a in zip(cols, acc):
            x2 = x1_ref[:, cs] + gate_ref[:, cs] * a
            o_ref[:, cs] = x2
            ssq = ssq + jnp.sum(x2 * x2, axis=-1, keepdims=True)
    o_ref[...] = o_ref[...] * lax.rsqrt(ssq * (1.0 / d) + NORM_EPS) * fg_ref[...]

    @pl.when(i + 1 == n)
    def _():
        wait_gathers(nxt)


def _combine(dest, ys, x1, ysh, w_tok, gate2, final_g, seq):
    t, d = x1.shape
    tr = _token_rows(d)
    tm = DEST_TOKENS
    assert t // tm >= 2, "the gather ring keeps two token tiles in flight"
    per_b = seq // tm
    row = pl.BlockSpec((tm, d), lambda i: (i, 0))
    return pl.pallas_call(
        _combine_kernel,
        grid=(t // tm,),
        in_specs=[pl.BlockSpec(memory_space=pl.ANY), pl.BlockSpec(memory_space=pl.ANY), row, row,
                  pl.BlockSpec((tm, TOP_K), lambda i: (i, 0)),
                  pl.BlockSpec((None, 1, d), lambda i: (i // per_b, 0, 0)),
                  pl.BlockSpec((1, d), lambda i: (0, 0))],
        out_specs=row,
        out_shape=jax.ShapeDtypeStruct((t, d), F32),
        scratch_shapes=[pltpu.VMEM((2, TOP_K, tm * tr, LANES), U32), pltpu.SMEM((2 * DEST_PER_TILE,), I32),
                        pltpu.SemaphoreType.DMA((2,)), pltpu.SemaphoreType.DMA((2,))],
        compiler_params=_params("arbitrary"),
        name="combine",
    )(dest, ys, x1, ysh, w_tok, gate2, final_g)


def _mixer(x2, mod6, cos_t, sin_t, bsz, seq, p):
    t, d = x2.shape
    shift1, scale1, gate1, shift2, scale2, _ = mod6
    d_rnn = p["conv_w"].shape[1]
    att_width = len(DILATION_GROUPS) * GROUP_COLS
    q_col = 2 * d_rnn
    gate_col = 2 * d_rnn + 3 * att_width
    w_in_b = p["w_in"].astype(BF16)

    h1 = _norm_mod(x2, p["norm1_g"].reshape(1, d), shift1, scale1, seq)
    xr = _proj_act(h1, w_in_b, 0, d_rnn, None, "proj_rnn_x")
    gr = _proj_act(h1, w_in_b, d_rnn, d_rnn, _gelu_tanh, "proj_rnn_gate")
    gates = _proj_act(h1, w_in_b, gate_col, 2 * d, _sigmoid_tanh, "proj_gates")
    ya = _rglru(xr, gr, p["conv_w"], p["conv_b"].reshape(1, d_rnn),
                p["rg_wa"].astype(BF16), p["rg_ba"].reshape(1, d_rnn),
                p["rg_wi"].astype(BF16), p["rg_bi"].reshape(1, d_rnn),
                p["rg_lambda"].reshape(1, d_rnn), bsz, seq, d_rnn)

    outs, lses = [], []
    for g, (window, dilation) in enumerate(DILATION_GROUPS):
        qkv = _proj_qkv(h1, w_in_b, cos_t, sin_t, g, dilation, bsz, seq, q_col, att_width)
        o, l = _attention_group(qkv, g, window, dilation)
        outs.append(o)
        lses.append(l)

    merged = _merge(ya, outs, lses, gates, p["w_proj_rnn"].astype(BF16), p["w_proj_attn"].astype(BF16), seq)
    return _out_proj(merged, p["w_out"].astype(BF16), x2, gate1, p["norm2_g"].reshape(1, d),
                     shift2, scale2, seq)


def _moe(h2, h2p, p):
    t, d = h2.shape
    ne = p["router_w"].shape[1]
    idx_t, w_t, rank_t, cnt = _router(h2, p["router_w"].T.astype(BF16), p["router_bias"].reshape(ne, 1))

    counts = cnt[:, 0].astype(I32)
    ends = jnp.cumsum(counts).astype(I32)
    starts = ends - counts
    n_rows = t * TOP_K
    first_blk = starts // EXPERT_ROWS
    n_blk_e = jnp.where(counts > 0, (ends - 1) // EXPERT_ROWS - first_blk + 1, 0)
    item_end = jnp.cumsum(n_blk_e).astype(I32)
    item_start = item_end - n_blk_e
    n_items = item_end[-1]
    max_items = n_rows // EXPERT_ROWS + ne
    w = jnp.minimum(jnp.arange(max_items, dtype=I32), n_items - 1)
    item_expert = jnp.minimum(jnp.searchsorted(item_end, w, side="right"), ne - 1).astype(I32)
    item_block = first_blk[item_expert] + (w - item_start[item_expert])
    after = item_end[item_expert]
    item_next = jnp.where(after < n_items, item_expert[jnp.minimum(after, max_items - 1)], -1).astype(I32)

    dest = _dest_rows(starts, idx_t, rank_t)
    dest = jnp.transpose(dest, (0, 2, 1)).reshape(t // DEST_TOKENS, DEST_PER_TILE)
    xs = _dispatch(dest, h2p, _token_rows(d))
    item_lo, item_hi = starts[item_expert], ends[item_expert]
    in_lo = jnp.maximum(item_lo, item_block * EXPERT_ROWS) - item_block * EXPERT_ROWS
    in_hi = jnp.minimum(item_hi, (item_block + 1) * EXPERT_ROWS) - item_block * EXPERT_ROWS
    item_half = jnp.where(in_hi <= EXPERT_ROWS // 2, 1, jnp.where(in_lo >= EXPERT_ROWS // 2, 2, 0)).astype(I32)
    ys = _experts(item_expert, item_block, item_lo, item_hi, item_next, item_half,
                  n_items.reshape(1), xs, p["exp_w1"], p["exp_w3"], p["exp_w2"])
    ysh = _shared(h2, p["sh_w1"].astype(BF16), p["sh_w3"].astype(BF16), p["sh_w2"].astype(BF16))
    return dest, ys, ysh, w_t.T


def kernel(x, c, positions, ada_w, ada_b, norm1_g, w_in, conv_w, conv_b, rg_wa, rg_ba, rg_wi, rg_bi, rg_lambda, w_proj_rnn, w_proj_attn, w_out, norm2_g, router_w, router_bias, exp_w1, exp_w3, exp_w2, sh_w1, sh_w3, sh_w2, final_g):
    bsz, seq, d = x.shape
    assert ada_w.shape[0] == 1, "the fused final norm assumes a single layer"
    t = bsz * seq
    x2 = x.reshape(t, d)
    first = lambda a: a.reshape(a.shape[1:])

    half = HEAD_DIM // 2
    inv_freq = ROPE_THETA ** (-jnp.arange(half, dtype=F32) * 2.0 / HEAD_DIM)
    freq = jnp.concatenate([inv_freq, inv_freq]).reshape(1, HEAD_DIM)
    sign = jnp.concatenate([-jnp.ones((half,), F32), jnp.ones((half,), F32)]).reshape(1, HEAD_DIM)
    cos_t, sin_t = _rope_tables(positions.reshape(t, 1), freq, sign)

    c_pad = jnp.zeros((SUBLANES, d), F32).at[:bsz].set(c)
    mod = _ada_mod(c_pad, first(ada_w), ada_b.reshape(1, -1))
    mod6 = tuple(mod[:bsz, k * d:(k + 1) * d].reshape(bsz, 1, d) for k in range(6))

    p = dict(norm1_g=first(norm1_g), w_in=first(w_in), conv_w=first(conv_w), conv_b=first(conv_b),
             rg_wa=first(rg_wa), rg_ba=first(rg_ba), rg_wi=first(rg_wi), rg_bi=first(rg_bi),
             rg_lambda=first(rg_lambda), w_proj_rnn=first(w_proj_rnn), w_proj_attn=first(w_proj_attn),
             w_out=first(w_out), norm2_g=first(norm2_g), router_w=first(router_w),
             router_bias=first(router_bias), exp_w1=first(exp_w1), exp_w3=first(exp_w3),
             exp_w2=first(exp_w2), sh_w1=first(sh_w1), sh_w3=first(sh_w3), sh_w2=first(sh_w2))
    x1, h2, h2p = _mixer(x2, mod6, cos_t, sin_t, bsz, seq, p)
    dest, ys, ysh, w_tok = _moe(h2, h2p, p)
    out = _combine(dest, ys, x1, ysh, w_tok, mod6[5], final_g.reshape(1, d), seq)
    return out.reshape(bsz, seq, d)
```

```python
import functools
import math

import jax
import jax.numpy as jnp
import numpy as np
from jax import lax
from jax.experimental import pallas as pl
from jax.experimental.pallas import tpu as pltpu

F32 = jnp.float32
BF16 = jnp.bfloat16
I32 = jnp.int32

HEAD_DIM = 128
HEADS_PER_GROUP = 4
DILATION_GROUPS = ((128, 1), (512, 4), (2048, 16))
ROPE_THETA = 10000.0
CONV_WIDTH = 4
LRU_C = 8.0
TOP_K = 8
N_EXPERT_GROUPS = 8
TOPK_GROUPS = 4
ROUTED_SCALE = 2.5
NORM_EPS = 1e-6

LANES = 128
SUBLANES = 8
VMEM_LIMIT_BYTES = 56 * 1024 * 1024

GROUP_COLS = HEADS_PER_GROUP * HEAD_DIM
EXPERT_ROWS = 256
DEST_TOKENS = 128


def _params(*sem):
    return pltpu.CompilerParams(dimension_semantics=sem, vmem_limit_bytes=VMEM_LIMIT_BYTES)


def _gelu_tanh(x):
    return 0.5 * x * (1.0 + jnp.tanh(math.sqrt(2.0 / math.pi) * (x + 0.044715 * (x * x * x))))


def _silu(x):
    return x * jax.nn.sigmoid(x)


def _rms(x, g):
    ms = jnp.mean(x * x, axis=-1, keepdims=True)
    return x * lax.rsqrt(ms + NORM_EPS) * g


def _ada_kernel(c_ref, w_ref, b_ref, o_ref):
    a = _silu(c_ref[...]).astype(BF16)
    o_ref[...] = jnp.dot(a, w_ref[...].astype(BF16), preferred_element_type=F32) + b_ref[...]


def _ada_mod(c_pad, ada_w, ada_b):
    rows, d = c_pad.shape
    n = ada_w.shape[1]
    tn = 1024
    return pl.pallas_call(
        _ada_kernel,
        grid=(n // tn,),
        in_specs=[
            pl.BlockSpec((rows, d), lambda j: (0, 0)),
            pl.BlockSpec((d, tn), lambda j: (0, j)),
            pl.BlockSpec((1, tn), lambda j: (0, j)),
        ],
        out_specs=pl.BlockSpec((rows, tn), lambda j: (0, j)),
        out_shape=jax.ShapeDtypeStruct((rows, n), F32),
        compiler_params=_params("arbitrary"),
        name="ada_mod",
    )(c_pad, ada_w, ada_b)


def _rope_kernel(pos_ref, freq_ref, sign_ref, cos_ref, sin_ref):
    ang = pos_ref[...].astype(F32) * freq_ref[...]
    cos_ref[...] = jnp.cos(ang)
    sin_ref[...] = jnp.sin(ang) * sign_ref[...]


def _rope_tables(pos_col, freq, sign):
    t = pos_col.shape[0]
    tm = 1024
    return pl.pallas_call(
        _rope_kernel,
        grid=(t // tm,),
        in_specs=[
            pl.BlockSpec((tm, 1), lambda i: (i, 0)),
            pl.BlockSpec((1, HEAD_DIM), lambda i: (0, 0)),
            pl.BlockSpec((1, HEAD_DIM), lambda i: (0, 0)),
        ],
        out_specs=[pl.BlockSpec((tm, HEAD_DIM), lambda i: (i, 0))] * 2,
        out_shape=[jax.ShapeDtypeStruct((t, HEAD_DIM), F32)] * 2,
        compiler_params=_params("arbitrary"),
        name="rope_tables",
    )(pos_col, freq, sign)


def _norm_kernel(x_ref, g_ref, sh_ref, sc_ref, h_ref):
    h_ref[...] = (_rms(x_ref[...], g_ref[...]) * (1.0 + sc_ref[...]) + sh_ref[...]).astype(h_ref.dtype)


def _norm_mod(x2, g, shift, scale, seq):
    t, d = x2.shape
    tm = 1024
    per_b = seq // tm
    bvec = pl.BlockSpec((None, 1, d), lambda i: (i // per_b, 0, 0))
    return pl.pallas_call(
        _norm_kernel,
        grid=(t // tm,),
        in_specs=[pl.BlockSpec((tm, d), lambda i: (i, 0)), pl.BlockSpec((1, d), lambda i: (0, 0)), bvec, bvec],
        out_specs=pl.BlockSpec((tm, d), lambda i: (i, 0)),
        out_shape=jax.ShapeDtypeStruct((t, d), BF16),
        compiler_params=_params("arbitrary"),
        name="norm_mod",
    )(x2, g, shift, scale)


def _proj_act_kernel(h_ref, w_ref, o_ref, *, act):
    acc = jnp.dot(h_ref[...], w_ref[...], preferred_element_type=F32)
    o_ref[...] = (acc if act is None else act(acc)).astype(o_ref.dtype)


def _sigmoid_tanh(x):
    return 0.5 + 0.5 * jnp.tanh(0.5 * x)


def _proj_act(h1, w_b, act, name):
    t, d = h1.shape
    ncols = w_b.shape[1]
    tm = 1024
    tn = 1024 if ncols % 1024 == 0 else GROUP_COLS
    return pl.pallas_call(
        functools.partial(_proj_act_kernel, act=act),
        grid=(t // tm, ncols // tn),
        in_specs=[pl.BlockSpec((tm, d), lambda i, j: (i, 0)),
                  pl.BlockSpec((d, tn), lambda i, j: (0, j))],
        out_specs=pl.BlockSpec((tm, tn), lambda i, j: (i, j)),
        out_shape=jax.ShapeDtypeStruct((t, ncols), BF16),
        compiler_params=_params("arbitrary", "arbitrary"),
        name=name,
    )(h1, w_b)


def _proj_qkv_kernel(h_ref, w_ref, cos_ref, sin_ref, o_ref, scr, *, dilation):
    tm = h_ref.shape[0]
    acc = jnp.dot(h_ref[...], w_ref[...], preferred_element_type=F32)
    j = pl.program_id(1)

    @pl.when(j < 2)
    def _():
        scale = jnp.where(j == 0, HEAD_DIM ** -0.5, 1.0).astype(F32)
        c = cos_ref[...] * scale
        s = sin_ref[...] * scale
        for h in range(HEADS_PER_GROUP):
            v = acc[:, h * HEAD_DIM:(h + 1) * HEAD_DIM]
            scr[h] = v * c + pltpu.roll(v, HEAD_DIM // 2, axis=1) * s

    @pl.when(j == 2)
    def _():
        for h in range(HEADS_PER_GROUP):
            scr[h] = acc[:, h * HEAD_DIM:(h + 1) * HEAD_DIM]

    sub = tm // dilation
    for r in range(dilation):
        for h in range(HEADS_PER_GROUP):
            rows = scr[h] if dilation == 1 else scr[h, pl.ds(r, sub, stride=dilation), :]
            o_ref[r, :, h * HEAD_DIM:(h + 1) * HEAD_DIM] = rows.astype(o_ref.dtype)


def _proj_qkv(h1, w_qkv_b, cos_t, sin_t, g, dilation, bsz, seq):
    t, d = h1.shape
    tm, tn = 1024, GROUP_COLS
    per_b = seq // tm
    sub = tm // dilation
    return pl.pallas_call(
        functools.partial(_proj_qkv_kernel, dilation=dilation),
        grid=(t // tm, 3),
        in_specs=[pl.BlockSpec((tm, d), lambda i, j: (i, 0)),
                  pl.BlockSpec((d, tn), lambda i, j: (0, j)),
                  pl.BlockSpec((tm, HEAD_DIM), lambda i, j: (i, 0)),
                  pl.BlockSpec((tm, HEAD_DIM), lambda i, j: (i, 0))],
        out_specs=pl.BlockSpec((None, dilation, sub, tn), lambda i, j: (i // per_b, 0, i % per_b, j)),
        out_shape=jax.ShapeDtypeStruct((bsz, dilation, seq // dilation, 3 * tn), BF16),
        scratch_shapes=[pltpu.VMEM((HEADS_PER_GROUP, tm, HEAD_DIM), F32)],
        compiler_params=_params("arbitrary", "arbitrary"),
        name=f"proj_qkv_g{g}",
    )(h1, w_qkv_b, cos_t, sin_t)


def _rglru_kernel(xr_ref, gr_ref, cw_ref, cb_ref, wa_ref, ba_ref, wi_ref, bi_ref, lam_ref,
                  ya_ref, xbuf, a_scr, b_scr, hcar):
    tt = xr_ref.shape[0]
    nc = xbuf.shape[0]
    halo = SUBLANES
    seg = tt // SUBLANES
    pitch = a_scr.shape[1] // SUBLANES
    lanes = lambda c: slice(c * LANES, (c + 1) * LANES)

    @pl.when(pl.program_id(2) == 0)
    def _():
        for c in range(nc):
            xbuf[c, 0:halo, :] = jnp.zeros((halo, LANES), F32)
        hcar[...] = jnp.zeros_like(hcar)

    for c in range(nc):
        cs = lanes(c)
        xbuf[c, halo:halo + tt, :] = xr_ref[:, cs].astype(F32)
        u = cb_ref[:, cs] + cw_ref[CONV_WIDTH - 1:CONV_WIDTH, cs] * xbuf[c, halo:halo + tt, :]
        for j in range(CONV_WIDTH - 1):
            s = CONV_WIDTH - 1 - j
            u = u + cw_ref[j:j + 1, cs] * xbuf[c, halo - s:halo - s + tt, :]
        xbuf[c, 0:halo, :] = xbuf[c, tt:tt + halo, :]

        ub = u.astype(BF16)
        r = 0.5 + 0.5 * jnp.tanh(0.5 * (jnp.dot(ub, wa_ref[c], preferred_element_type=F32) + ba_ref[:, cs]))
        ig = 0.5 + 0.5 * jnp.tanh(0.5 * (jnp.dot(ub, wi_ref[c], preferred_element_type=F32) + bi_ref[:, cs]))
        z = -lam_ref[:, cs]
        softplus = jnp.maximum(z, 0.0) + jnp.log1p(jnp.exp(-jnp.abs(z)))
        t = jnp.tanh((-0.5 * LRU_C) * r * softplus)
        q = 1.0 / (1.0 - t)
        a = (1.0 + t) * q
        b = (2.0 * q) * jnp.sqrt(-t) * (ig * u)
        for s in range(SUBLANES):
            a_scr[c, s * pitch:s * pitch + seg, :] = a[s * seg:(s + 1) * seg, :]
            b_scr[c, s * pitch:s * pitch + seg, :] = b[s * seg:(s + 1) * seg, :]

    h = [jnp.zeros((SUBLANES, LANES), F32)] * nc
    prod = [jnp.ones((SUBLANES, LANES), F32)] * nc
    for j in range(seg):
        rows_j = pl.ds(j, SUBLANES, stride=pitch)
        for c in range(nc):
            aj = a_scr[c, rows_j, :]
            h[c] = aj * h[c] + b_scr[c, rows_j, :]
            prod[c] = aj * prod[c]
            b_scr[c, rows_j, :] = h[c]
            a_scr[c, rows_j, :] = prod[c]
    start = []
    for c in range(nc):
        carry = hcar[0:1, lanes(c)]
        entering = []
        for s in range(SUBLANES):
            entering.append(carry)
            carry = h[c][s:s + 1, :] + prod[c][s:s + 1, :] * carry
        hcar[:, lanes(c)] = jnp.broadcast_to(carry, (SUBLANES, LANES))
        start.append(jnp.concatenate(entering, axis=0))
    for j in range(seg):
        rows_j = pl.ds(j, SUBLANES, stride=pitch)
        for c in range(nc):
            b_scr[c, rows_j, :] = b_scr[c, rows_j, :] + a_scr[c, rows_j, :] * start[c]
    for c in range(nc):
        for s in range(SUBLANES):
            rs = slice(s * seg, (s + 1) * seg)
            gate = gr_ref[rs, lanes(c)].astype(F32)
            ya_ref[rs, lanes(c)] = (b_scr[c, s * pitch:s * pitch + seg, :] * gate).astype(ya_ref.dtype)


RGLRU_CHANNEL_TILES = 2


def _rglru(xr, gr, conv_w, conv_b, wa_b, ba, wi_b, bi, lam, bsz, seq, d_rnn):
    t = xr.shape[0]
    tt = 512
    nc = RGLRU_CHANNEL_TILES
    width = nc * LANES
    per_b = seq // tt
    row = pl.BlockSpec((tt, width), lambda b, c, s: (b * per_b + s, c))
    vec = pl.BlockSpec((1, width), lambda b, c, s: (0, c))
    gate_w = pl.BlockSpec((nc, LANES, LANES), lambda b, c, s: (c, 0, 0))
    scan = pltpu.VMEM((nc, tt + SUBLANES * SUBLANES, LANES), F32)
    return pl.pallas_call(
        _rglru_kernel,
        grid=(bsz, d_rnn // width, per_b),
        in_specs=[row, row, pl.BlockSpec((CONV_WIDTH, width), lambda b, c, s: (0, c)), vec,
                  gate_w, vec, gate_w, vec, vec],
        out_specs=row,
        out_shape=jax.ShapeDtypeStruct((t, d_rnn), BF16),
        scratch_shapes=[pltpu.VMEM((nc, tt + SUBLANES, LANES), F32), scan, scan, pltpu.VMEM((SUBLANES, width), F32)],
        compiler_params=_params("arbitrary", "arbitrary", "arbitrary"),
        name="rglru",
    )(xr, gr, conv_w, conv_b, wa_b, ba, wi_b, bi, lam)


ATTN_BLOCKS = 4


def _attn_kernel(q_ref, kc_ref, kp_ref, vc_ref, vp_ref, o_ref, l_ref, *, blk):
    nq = q_ref.shape[0] // blk
    not_first = pl.program_id(2) > 0
    qi = lax.broadcasted_iota(I32, (blk, blk), 0)
    kj = lax.broadcasted_iota(I32, (blk, blk), 1)
    tri_prev = kj >= qi
    mask_cur = kj <= qi
    nt = (((1,), (1,)), ((), ()))
    units = [(j, h) for j in range(nq) for h in range(HEADS_PER_GROUP)]
    rows = lambda j: slice(j * blk, (j + 1) * blk)
    cols = lambda h: slice(h * HEAD_DIM, (h + 1) * HEAD_DIM)

    def prev_kv(ref, pref, j, h):
        return pref[:, cols(h)] if j == 0 else ref[rows(j - 1), cols(h)]

    scores = []
    for j, h in units:
        q = q_ref[rows(j), cols(h)]
        sp = lax.dot_general(q, prev_kv(kc_ref, kp_ref, j, h), nt, preferred_element_type=F32)
        sc = lax.dot_general(q, kc_ref[rows(j), cols(h)], nt, preferred_element_type=F32)
        mask_prev = jnp.logical_and(tri_prev, not_first) if j == 0 else tri_prev
        scores.append((jnp.where(mask_prev, sp, -jnp.inf), jnp.where(mask_cur, sc, -jnp.inf)))
    maxes = [jnp.maximum(jnp.max(sp, axis=-1, keepdims=True), jnp.max(sc, axis=-1, keepdims=True))
             for sp, sc in scores]
    probs = [(jnp.exp(sp - m), jnp.exp(sc - m)) for (sp, sc), m in zip(scores, maxes)]
    dens = [jnp.sum(pp, axis=-1, keepdims=True) + jnp.sum(pc, axis=-1, keepdims=True) for pp, pc in probs]
    for (j, h), (pp, pc), m, den in zip(units, probs, maxes, dens):
        inv = 1.0 / den
        out = (jnp.dot((pp * inv).astype(BF16), prev_kv(vc_ref, vp_ref, j, h), preferred_element_type=F32)
               + jnp.dot((pc * inv).astype(BF16), vc_ref[rows(j), cols(h)], preferred_element_type=F32))
        o_ref[rows(j), cols(h)] = out.astype(o_ref.dtype)
        l_ref[rows(j), cols(h)] = jnp.broadcast_to(m + jnp.log(den), (blk, HEAD_DIM))


def _attention_group(qkv, g, window, dilation):
    bsz, _, length, _ = qkv.shape
    blk = window // dilation
    nb = length // blk
    nq = math.gcd(ATTN_BLOCKS, nb)
    cur = lambda c: (lambda b, r, n: (b, r, n, c))
    prev = lambda c: (lambda b, r, n: (b, r, jnp.maximum(n * nq - 1, 0), c))
    spec = lambda f: pl.BlockSpec((None, None, nq * blk, GROUP_COLS), f)
    pspec = lambda f: pl.BlockSpec((None, None, blk, GROUP_COLS), f)
    out_sds = lambda dt: jax.ShapeDtypeStruct((bsz, dilation, length, GROUP_COLS), dt)
    return pl.pallas_call(
        functools.partial(_attn_kernel, blk=blk),
        grid=(bsz, dilation, nb // nq),
        in_specs=[spec(cur(0)), spec(cur(1)), pspec(prev(1)), spec(cur(2)), pspec(prev(2))],
        out_specs=[spec(cur(0)), spec(cur(0))],
        out_shape=[out_sds(BF16), out_sds(F32)],
        compiler_params=_params("arbitrary", "arbitrary", "arbitrary"),
        name=f"attn_g{g}",
    )(qkv, qkv, qkv, qkv, qkv)


def _merge_kernel(ya_ref, o0, o1, o2, l0, l1, l2, g_ref, wr_ref, wa_ref, m_ref, yb_scr, o_scr, l_scr):
    tm = ya_ref.shape[0]
    d = m_ref.shape[1]
    for g, (o_ref, l_ref) in enumerate(((o0, l0), (o1, l1), (o2, l2))):
        dil = o_ref.shape[0]
        for r in range(dil):
            for h in range(HEADS_PER_GROUP):
                cs = slice(h * HEAD_DIM, (h + 1) * HEAD_DIM)
                if dil == 1:
                    o_scr[g, h] = o_ref[r, :, cs].astype(F32)
                    l_scr[g, h] = l_ref[r, :, cs]
                else:
                    o_scr[g, h, pl.ds(r, tm // dil, stride=dil), :] = o_ref[r, :, cs].astype(F32)
                    l_scr[g, h, pl.ds(r, tm // dil, stride=dil), :] = l_ref[r, :, cs]
    for h in range(HEADS_PER_GROUP):
        la, lb, lc = l_scr[0, h], l_scr[1, h], l_scr[2, h]
        m = jnp.maximum(jnp.maximum(la, lb), lc)
        ea, eb, ec = jnp.exp(la - m), jnp.exp(lb - m), jnp.exp(lc - m)
        inv = 1.0 / (ea + eb + ec)
        yb = (ea * inv) * o_scr[0, h] + (eb * inv) * o_scr[1, h] + (ec * inv) * o_scr[2, h]
        yb_scr[:, h * HEAD_DIM:(h + 1) * HEAD_DIM] = yb.astype(BF16)

    ya = ya_ref[...]
    yb = yb_scr[...]
    for c in range(d // GROUP_COLS):
        cs = slice(c * GROUP_COLS, (c + 1) * GROUP_COLS)
        gs = slice(d + c * GROUP_COLS, d + (c + 1) * GROUP_COLS)
        pa = jnp.dot(ya, wr_ref[:, cs], preferred_element_type=F32)
        pb = jnp.dot(yb, wa_ref[:, cs], preferred_element_type=F32)
        m_ref[:, cs] = (g_ref[:, cs].astype(F32) * pa + g_ref[:, gs].astype(F32) * pb).astype(m_ref.dtype)


def _merge(ya, outs, lses, gates, wr_b, wa_b, seq):
    t, d_rnn = ya.shape
    d = wr_b.shape[1]
    tm = 512
    per_b = seq // tm
    n_groups = len(outs)
    resident = lambda shape: pl.BlockSpec(shape, lambda i: (0, 0), pipeline_mode=pl.Buffered(1))

    def grp(o):
        dil = o.shape[1]
        return pl.BlockSpec((None, dil, tm // dil, GROUP_COLS), lambda i: (i // per_b, 0, i % per_b, 0))

    return pl.pallas_call(
        _merge_kernel,
        grid=(t // tm,),
        in_specs=[
            pl.BlockSpec((tm, d_rnn), lambda i: (i, 0)),
            *[grp(o) for o in outs], *[grp(l) for l in lses],
            pl.BlockSpec((tm, 2 * d), lambda i: (i, 0)),
            resident((d_rnn, d)), resident((GROUP_COLS, d)),
        ],
        out_specs=pl.BlockSpec((tm, d), lambda i: (i, 0)),
        out_shape=jax.ShapeDtypeStruct((t, d), BF16),
        scratch_shapes=[pltpu.VMEM((tm, GROUP_COLS), BF16),
                        pltpu.VMEM((n_groups, HEADS_PER_GROUP, tm, HEAD_DIM), F32),
                        pltpu.VMEM((n_groups, HEADS_PER_GROUP, tm, HEAD_DIM), F32)],
        compiler_params=_params("arbitrary"),
        name="merge_proj",
    )(ya, *outs, *lses, gates, wr_b, wa_b)


U32 = jnp.uint32
HIGH_HALF = np.uint32(0xFFFF0000)


def _token_rows(d):
    assert d % (2 * LANES) == 0
    return d // (2 * LANES)


def _pack_rows(v, dst_ref):
    rows, d = v.shape
    tr = _token_rows(d)
    lo = pltpu.bitcast(v[:, :d // 2].astype(BF16).astype(F32), U32)
    hi = pltpu.bitcast(v[:, d // 2:].astype(BF16).astype(F32), U32)
    word = (lo >> 16) | (hi & HIGH_HALF)
    for s in range(tr):
        dst_ref[pl.ds(s, rows, stride=tr), :] = word[:, s * LANES:(s + 1) * LANES]


def _unpack_slab(src_ref, s, rows, tr):
    word = src_ref[pl.ds(s, rows, stride=tr), :]
    return pltpu.bitcast(word << 16, F32), pltpu.bitcast(word & HIGH_HALF, F32)


def _outproj_kernel(m_ref, w_ref, x_ref, gate_ref, g2_ref, sh_ref, sc_ref, x1_ref, h2_ref, h2p_ref):
    x1 = x_ref[...] + gate_ref[...] * jnp.dot(m_ref[...], w_ref[...], preferred_element_type=F32)
    x1_ref[...] = x1
    h2 = _rms(x1, g2_ref[...]) * (1.0 + sc_ref[...]) + sh_ref[...]
    h2_ref[...] = h2.astype(h2_ref.dtype)
    _pack_rows(h2, h2p_ref)


def _out_proj(merged, w_out_b, x2, gate1, g2, shift2, scale2, seq):
    t, d = x2.shape
    tr = _token_rows(d)
    tm = 512
    per_b = seq // tm
    row = pl.BlockSpec((tm, d), lambda i: (i, 0))
    bvec = pl.BlockSpec((None, 1, d), lambda i: (i // per_b, 0, 0))
    return pl.pallas_call(
        _outproj_kernel,
        grid=(t // tm,),
        in_specs=[row, pl.BlockSpec((d, d), lambda i: (0, 0)), row, bvec,
                  pl.BlockSpec((1, d), lambda i: (0, 0)), bvec, bvec],
        out_specs=[row, row, pl.BlockSpec((tm * tr, LANES), lambda i: (i, 0))],
        out_shape=[jax.ShapeDtypeStruct((t, d), F32), jax.ShapeDtypeStruct((t, d), BF16),
                   jax.ShapeDtypeStruct((t * tr, LANES), U32)],
        compiler_params=_params("arbitrary"),
        name="out_proj",
    )(merged, w_out_b, x2, gate1, g2, shift2, scale2)


def _router_kernel(h_ref, rw_ref, bias_ref, idx_ref, w_ref, rank_ref, cnt_ref, carry):
    ne = rw_ref.shape[0]
    tm = h_ref.shape[0]
    gsz = ne // N_EXPERT_GROUPS

    @pl.when(pl.program_id(0) == 0)
    def _():
        carry[...] = jnp.zeros_like(carry)

    logits = lax.dot_general(rw_ref[...], h_ref[...].astype(BF16), (((1,), (1,)), ((), ())),
                             preferred_element_type=F32)
    scores = jax.nn.sigmoid(logits)
    sel = scores + bias_ref[...]
    row = lax.broadcasted_iota(I32, (ne, tm), 0)
    neg = -jnp.inf

    gscore = []
    rg = lax.broadcasted_iota(I32, (gsz, tm), 0)
    for g in range(N_EXPERT_GROUPS):
        sg = sel[g * gsz:(g + 1) * gsz, :]
        m1 = jnp.max(sg, axis=0, keepdims=True)
        i1 = jnp.min(jnp.where(sg == m1, rg, ne), axis=0, keepdims=True)
        m2 = jnp.max(jnp.where(rg == i1, neg, sg), axis=0, keepdims=True)
        gscore.append(m1 + m2)
    keep_rows = []
    for g in range(N_EXPERT_GROUPS):
        beaten = jnp.zeros((1, tm), I32)
        for o in range(N_EXPERT_GROUPS):
            if o == g:
                continue
            wins = (gscore[o] >= gscore[g]) if o < g else (gscore[o] > gscore[g])
            beaten = beaten + wins.astype(I32)
        keep_rows.append(jnp.broadcast_to(beaten, (gsz, tm)))
    cur = jnp.where(jnp.concatenate(keep_rows, axis=0) < TOPK_GROUPS, sel, neg)

    chosen = jnp.zeros((ne, tm), F32)
    picks, wts = [], []
    for _ in range(TOP_K):
        m = jnp.max(cur, axis=0, keepdims=True)
        ik = jnp.min(jnp.where(cur == m, row, ne), axis=0, keepdims=True)
        hit = row == ik
        wts.append(jnp.sum(jnp.where(hit, scores, 0.0), axis=0, keepdims=True))
        cur = jnp.where(hit, neg, cur)
        chosen = jnp.where(hit, 1.0, chosen)
        picks.append(ik)
    wsum = wts[0]
    for k in range(1, TOP_K):
        wsum = wsum + wts[k]

    ti = lax.broadcasted_iota(I32, (tm, tm), 0)
    tj = lax.broadcasted_iota(I32, (tm, tm), 1)
    upper = (ti < tj).astype(BF16)
    chosen_b = chosen.astype(BF16)
    before = jnp.dot(chosen_b, upper, preferred_element_type=F32)
    total = jnp.dot(chosen_b, jnp.ones((tm, LANES), BF16), preferred_element_type=F32)
    base = carry[...]
    pos = before + jnp.concatenate([base] * (tm // LANES), axis=1)
    for k in range(TOP_K):
        hit = row == picks[k]
        idx_ref[k:k + 1, :] = picks[k]
        w_ref[k:k + 1, :] = wts[k] / wsum * ROUTED_SCALE
        rank_ref[k:k + 1, :] = jnp.sum(jnp.where(hit, pos, 0.0), axis=0, keepdims=True).astype(I32)
    carry[...] = base + total
    cnt_ref[...] = base + total


def _router(h2, rw_t, bias_col):
    t, d = h2.shape
    ne = rw_t.shape[0]
    tm = 256
    kt = pl.BlockSpec((TOP_K, tm), lambda i: (0, i))
    return pl.pallas_call(
        _router_kernel,
        grid=(t // tm,),
        in_specs=[pl.BlockSpec((tm, d), lambda i: (i, 0)),
                  pl.BlockSpec((ne, d), lambda i: (0, 0)),
                  pl.BlockSpec((ne, 1), lambda i: (0, 0))],
        out_specs=[kt, kt, kt, pl.BlockSpec((ne, LANES), lambda i: (0, 0))],
        out_shape=[jax.ShapeDtypeStruct((TOP_K, t), I32), jax.ShapeDtypeStruct((TOP_K, t), F32),
                   jax.ShapeDtypeStruct((TOP_K, t), I32), jax.ShapeDtypeStruct((ne, LANES), F32)],
        scratch_shapes=[pltpu.VMEM((ne, LANES), F32)],
        compiler_params=_params("arbitrary"),
        name="router",
    )(h2, rw_t, bias_col)


def _dest_kernel(start_ref, idx_ref, rank_ref, dest_ref):
    ne = start_ref.shape[0]
    idx = idx_ref[...]

    def body(e, acc):
        return jnp.where(idx == e, start_ref[e], acc)

    dest = rank_ref[...] + lax.fori_loop(0, ne, body, jnp.zeros(idx.shape, I32))
    for j in range(dest_ref.shape[0]):
        dest_ref[j] = dest[:, j * DEST_TOKENS:(j + 1) * DEST_TOKENS]


def _dest_rows(starts, idx_t, rank_t):
    t = idx_t.shape[1]
    tb = 2048
    per = tb // DEST_TOKENS
    return pl.pallas_call(
        _dest_kernel,
        grid_spec=pltpu.PrefetchScalarGridSpec(
            num_scalar_prefetch=1,
            grid=(t // tb,),
            in_specs=[pl.BlockSpec((TOP_K, tb), lambda i, s: (0, i)),
                      pl.BlockSpec((TOP_K, tb), lambda i, s: (0, i))],
            out_specs=pl.BlockSpec((per, TOP_K, DEST_TOKENS), lambda i, s: (i, 0, 0)),
        ),
        out_shape=jax.ShapeDtypeStruct((t // DEST_TOKENS, TOP_K, DEST_TOKENS), I32),
        compiler_params=_params("arbitrary"),
        name="dest_rows",
    )(starts, idx_t, rank_t)


ROW_DMA_UNROLL = 4
DEST_PER_TILE = DEST_TOKENS * TOP_K


def _idx_copy(dest_hbm, idx_smem, isem, tile, s):
    return pltpu.make_async_copy(
        dest_hbm.at[tile], idx_smem.at[pl.ds(pl.multiple_of(s * DEST_PER_TILE, DEST_PER_TILE), DEST_PER_TILE)],
        isem.at[s])


def _token_rows_at(ref, token, tr):
    return ref.at[pl.ds(pl.multiple_of(token * tr, tr), tr), :]


def _dispatch_kernel(dest_hbm, h_ref, xs_hbm, idx_smem, isem, dsem, *, tr):
    i = pl.program_id(0)
    n = pl.num_programs(0)
    tm = h_ref.shape[0] // tr
    slot = i % 2
    idx_copy = functools.partial(_idx_copy, dest_hbm, idx_smem, isem)

    @pl.when(i == 0)
    def _():
        idx_copy(0, 0).start()

    idx_copy(i, slot).wait()

    @pl.when(i + 1 < n)
    def _():
        idx_copy(i + 1, 1 - slot).start()

    base = slot * DEST_PER_TILE

    def body(c, carry):
        for u in range(ROW_DMA_UNROLL):
            t = c * ROW_DMA_UNROLL + u
            for k in range(TOP_K):
                d = idx_smem[base + t * TOP_K + k]
                pltpu.make_async_copy(_token_rows_at(h_ref, t, tr), _token_rows_at(xs_hbm, d, tr),
                                      dsem).start(priority=k % 2)
        return carry

    lax.fori_loop(0, tm // ROW_DMA_UNROLL, body, 0)
    for k in range(TOP_K):
        pltpu.make_async_copy(h_ref, xs_hbm.at[pl.ds(0, tm * tr), :], dsem).wait()


def _dispatch(dest, h2p, tr):
    t = h2p.shape[0] // tr
    tm = DEST_TOKENS
    return pl.pallas_call(
        functools.partial(_dispatch_kernel, tr=tr),
        grid=(t // tm,),
        in_specs=[pl.BlockSpec(memory_space=pl.ANY), pl.BlockSpec((tm * tr, LANES), lambda i: (i, 0))],
        out_specs=pl.BlockSpec(memory_space=pl.ANY),
        out_shape=jax.ShapeDtypeStruct((t * TOP_K * tr, LANES), U32),
        scratch_shapes=[pltpu.SMEM((2 * DEST_PER_TILE,), I32), pltpu.SemaphoreType.DMA((2,)),
                        pltpu.SemaphoreType.DMA(())],
        compiler_params=_params("arbitrary"),
        name="dispatch",
    )(dest, h2p)


def _experts_kernel(e_ref, b_ref, lo_ref, hi_ref, nxt_ref, half_ref, n_ref, xs_ref, w1_hbm, w3_hbm, w2_hbm, ys_ref,
                    w1s, w3s, w2s, w1b, w3b, w2b, xb, yp, wsem):
    w = pl.program_id(0)
    prev = jnp.maximum(w - 1, 0)

    def fetch(e):
        return (pltpu.make_async_copy(w1_hbm.at[e], w1s, wsem.at[0]),
                pltpu.make_async_copy(w3_hbm.at[e], w3s, wsem.at[1]),
                pltpu.make_async_copy(w2_hbm.at[e], w2s, wsem.at[2]))

    @pl.when(w < n_ref[0])
    def _():
        @pl.when(w == 0)
        def _():
            for cp in fetch(e_ref[0]):
                cp.start()

        @pl.when(jnp.logical_or(w == 0, e_ref[w] != e_ref[prev]))
        def _():
            for cp in fetch(e_ref[w]):
                cp.wait()
            w1b[...] = w1s[...].astype(BF16)
            w3b[...] = w3s[...].astype(BF16)
            w2b[...] = w2s[...].astype(BF16)

            @pl.when(nxt_ref[w] >= 0)
            def _():
                for cp in fetch(nxt_ref[w]):
                    cp.start()

        d = xb.shape[1]
        tr = _token_rows(d)
        new_block = jnp.logical_or(w == 0, b_ref[w] != b_ref[prev])

        @pl.when(new_block)
        def _():
            for s in range(tr):
                lo, hi = _unpack_slab(xs_ref, s, EXPERT_ROWS, tr)
                xb[:, s * LANES:(s + 1) * LANES] = lo.astype(BF16)
                xb[:, d // 2 + s * LANES:d // 2 + (s + 1) * LANES] = hi.astype(BF16)
            ys_ref[...] = jnp.zeros(ys_ref.shape, U32)

        first = (lo_ref[w] - b_ref[w] * EXPERT_ROWS) * tr
        last = (hi_ref[w] - b_ref[w] * EXPERT_ROWS) * tr

        def run(row0, nrows):
            x = xb[pl.ds(row0, nrows), :]
            h1 = jnp.dot(x, w1b[...], preferred_element_type=F32)
            h3 = jnp.dot(x, w3b[...], preferred_element_type=F32)
            act = (_silu(h1) * h3).astype(BF16)
            span = pl.ds(row0 * tr, nrows * tr)
            _pack_rows(jnp.dot(act, w2b[...], preferred_element_type=F32), yp.at[span, :])
            prow = row0 * tr + lax.broadcasted_iota(I32, (nrows * tr, 1), 0)
            mine = jnp.logical_and(prow >= first, prow < last)
            ys_ref[span, :] = jnp.where(mine, yp[span, :], ys_ref[span, :])

        half_rows = EXPERT_ROWS // 2

        @pl.when(half_ref[w] == 0)
        def _():
            run(0, EXPERT_ROWS)

        @pl.when(half_ref[w] != 0)
        def _():
            run(pl.multiple_of((half_ref[w] - 1) * half_rows, half_rows), half_rows)


def _experts(item_expert, item_block, item_lo, item_hi, item_next, item_half, n_items, xs, w1, w3, w2):
    _, d, de = w1.shape
    tr = _token_rows(d)
    rows = lambda w, e, b, lo, hi, nx, hf, n: (b[w], 0)
    hbm = pl.BlockSpec(memory_space=pl.ANY)
    return pl.pallas_call(
        _experts_kernel,
        grid_spec=pltpu.PrefetchScalarGridSpec(
            num_scalar_prefetch=7,
            grid=(item_expert.shape[0],),
            in_specs=[pl.BlockSpec((EXPERT_ROWS * tr, LANES), rows), hbm, hbm, hbm],
            out_specs=pl.BlockSpec((EXPERT_ROWS * tr, LANES), rows),
            scratch_shapes=[pltpu.VMEM((d, de), F32), pltpu.VMEM((d, de), F32), pltpu.VMEM((de, d), F32),
                            pltpu.VMEM((d, de), BF16), pltpu.VMEM((d, de), BF16), pltpu.VMEM((de, d), BF16),
                            pltpu.VMEM((EXPERT_ROWS, d), BF16), pltpu.VMEM((EXPERT_ROWS * tr, LANES), U32),
                            pltpu.SemaphoreType.DMA((3,))],
        ),
        out_shape=jax.ShapeDtypeStruct(xs.shape, U32),
        compiler_params=_params("arbitrary"),
        name="experts",
    )(item_expert, item_block, item_lo, item_hi, item_next, item_half, n_items, xs, w1, w3, w2)


def _shared_kernel(h_ref, w1_ref, w3_ref, w2_ref, y_ref):
    x = h_ref[...].astype(BF16)
    h1 = jnp.dot(x, w1_ref[...], preferred_element_type=F32)
    h3 = jnp.dot(x, w3_ref[...], preferred_element_type=F32)
    y_ref[...] = jnp.dot((_silu(h1) * h3).astype(BF16), w2_ref[...], preferred_element_type=F32)


def _shared(h2, w1_b, w3_b, w2_b):
    t, d = h2.shape
    de = w1_b.shape[1]
    tm = 512
    row = pl.BlockSpec((tm, d), lambda i: (i, 0))
    return pl.pallas_call(
        _shared_kernel,
        grid=(t // tm,),
        in_specs=[row, pl.BlockSpec((d, de), lambda i: (0, 0)), pl.BlockSpec((d, de), lambda i: (0, 0)),
                  pl.BlockSpec((de, d), lambda i: (0, 0))],
        out_specs=row,
        out_shape=jax.ShapeDtypeStruct((t, d), F32),
        compiler_params=_params("arbitrary"),
        name="shared_expert",
    )(h2, w1_b, w3_b, w2_b)


def _combine_kernel(dest_hbm, ys_hbm, x1_ref, ysh_ref, wt_ref, gate_ref, fg_ref, o_ref,
                    rows, idx_smem, isem, gsem):
    i = pl.program_id(0)
    n = pl.num_programs(0)
    tm, d = x1_ref.shape
    tr = _token_rows(d)
    slot = i % 2
    idx_copy = functools.partial(_idx_copy, dest_hbm, idx_smem, isem)

    def issue_gathers(s):
        base = s * DEST_PER_TILE

        def body(c, carry):
            for u in range(ROW_DMA_UNROLL):
                t = c * ROW_DMA_UNROLL + u
                for k in range(TOP_K):
                    src = idx_smem[base + t * TOP_K + k]
                    pltpu.make_async_copy(_token_rows_at(ys_hbm, src, tr), _token_rows_at(rows.at[s, k], t, tr),
                                          gsem.at[s]).start(priority=k % 2)
            return carry

        lax.fori_loop(0, tm // ROW_DMA_UNROLL, body, 0)

    @pl.when(i == 0)
    def _():
        idx_copy(0, 0).start()
        idx_copy(0, 0).wait()
        issue_gathers(0)

        @pl.when(n > 1)
        def _():
            idx_copy(1, 1).start()

    nxt = 1 - slot

    @pl.when(i + 1 < n)
    def _():
        idx_copy(i + 1, nxt).wait()

    @pl.when(i + 2 < n)
    def _():
        idx_copy(i + 2, slot).start()

    def wait_gathers(s):
        for k in range(TOP_K):
            pltpu.make_async_copy(ys_hbm.at[pl.ds(0, tm * tr), :], rows.at[s, k], gsem.at[s]).wait()

    wait_gathers(slot)

    def issue_next(t0, t1):
        for t in range(t0, t1):
            for k in range(TOP_K):
                src = idx_smem[nxt * DEST_PER_TILE + t * TOP_K + k]
                pltpu.make_async_copy(_token_rows_at(ys_hbm, src, tr), rows.at[nxt, k, pl.ds(t * tr, tr), :],
                                      gsem.at[nxt]).start(priority=k % 2)

    w = wt_ref[...]
    ssq = jnp.zeros((tm, 1), F32)
    for s in range(tr):
        issue_next(s * tm // tr, (s + 1) * tm // tr)
        cols = (slice(s * LANES, (s + 1) * LANES), slice(d // 2 + s * LANES, d // 2 + (s + 1) * LANES))
        acc = [ysh_ref[:, cs] for cs in cols]
        for k in range(TOP_K):
            halves = _unpack_slab(rows.at[slot, k], s, tm, tr)
            acc = [a + v * w[:, k:k + 1] for a, v in zip(acc, halves)]
        for cs, a in zip(cols, acc):
            x2 = x1_ref[:, cs] + gate_ref[:, cs] * a
            o_ref[:, cs] = x2
            ssq = ssq + jnp.sum(x2 * x2, axis=-1, keepdims=True)
    o_ref[...] = o_ref[...] * lax.rsqrt(ssq * (1.0 / d) + NORM_EPS) * fg_ref[...]

    @pl.when(i + 1 == n)
    def _():
        wait_gathers(nxt)


def _combine(dest, ys, x1, ysh, w_tok, gate2, final_g, seq):
    t, d = x1.shape
    tr = _token_rows(d)
    tm = DEST_TOKENS
    assert t // tm >= 2, "the gather ring keeps two token tiles in flight"
    per_b = seq // tm
    row = pl.BlockSpec((tm, d), lambda i: (i, 0))
    return pl.pallas_call(
        _combine_kernel,
        grid=(t // tm,),
        in_specs=[pl.BlockSpec(memory_space=pl.ANY), pl.BlockSpec(memory_space=pl.ANY), row, row,
                  pl.BlockSpec((tm, TOP_K), lambda i: (i, 0)),
                  pl.BlockSpec((None, 1, d), lambda i: (i // per_b, 0, 0)),
                  pl.BlockSpec((1, d), lambda i: (0, 0))],
        out_specs=row,
        out_shape=jax.ShapeDtypeStruct((t, d), F32),
        scratch_shapes=[pltpu.VMEM((2, TOP_K, tm * tr, LANES), U32), pltpu.SMEM((2 * DEST_PER_TILE,), I32),
                        pltpu.SemaphoreType.DMA((2,)), pltpu.SemaphoreType.DMA((2,))],
        compiler_params=_params("arbitrary"),
        name="combine",
    )(dest, ys, x1, ysh, w_tok, gate2, final_g)


def _mixer(x2, mod6, cos_t, sin_t, bsz, seq, p):
    t, d = x2.shape
    shift1, scale1, gate1, shift2, scale2, _ = mod6
    d_rnn = p["conv_w"].shape[1]
    att_width = len(DILATION_GROUPS) * GROUP_COLS
    q_col = 2 * d_rnn
    gate_col = 2 * d_rnn + 3 * att_width

    def qkv_weights(g):
        parts = [p["w_in"][:, q_col + j * att_width + g * GROUP_COLS:q_col + j * att_width + (g + 1) * GROUP_COLS]
                 for j in range(3)]
        return jnp.concatenate(parts, axis=1).astype(BF16)

    h1 = _norm_mod(x2, p["norm1_g"].reshape(1, d), shift1, scale1, seq)
    xr = _proj_act(h1, p["w_in"][:, :d_rnn].astype(BF16), None, "proj_rnn_x")
    gr = _proj_act(h1, p["w_in"][:, d_rnn:2 * d_rnn].astype(BF16), _gelu_tanh, "proj_rnn_gate")
    gates = _proj_act(h1, p["w_in"][:, gate_col:].astype(BF16), _sigmoid_tanh, "proj_gates")
    ya = _rglru(xr, gr, p["conv_w"], p["conv_b"].reshape(1, d_rnn),
                p["rg_wa"].astype(BF16), p["rg_ba"].reshape(1, d_rnn),
                p["rg_wi"].astype(BF16), p["rg_bi"].reshape(1, d_rnn),
                p["rg_lambda"].reshape(1, d_rnn), bsz, seq, d_rnn)

    outs, lses = [], []
    for g, (window, dilation) in enumerate(DILATION_GROUPS):
        qkv = _proj_qkv(h1, qkv_weights(g), cos_t, sin_t, g, dilation, bsz, seq)
        o, l = _attention_group(qkv, g, window, dilation)
        outs.append(o)
        lses.append(l)

    merged = _merge(ya, outs, lses, gates, p["w_proj_rnn"].astype(BF16), p["w_proj_attn"].astype(BF16), seq)
    return _out_proj(merged, p["w_out"].astype(BF16), x2, gate1, p["norm2_g"].reshape(1, d),
                     shift2, scale2, seq)


def _moe(h2, h2p, p):
    t, d = h2.shape
    ne = p["router_w"].shape[1]
    idx_t, w_t, rank_t, cnt = _router(h2, p["router_w"].T.astype(BF16), p["router_bias"].reshape(ne, 1))

    counts = cnt[:, 0].astype(I32)
    ends = jnp.cumsum(counts).astype(I32)
    starts = ends - counts
    n_rows = t * TOP_K
    first_blk = starts // EXPERT_ROWS
    n_blk_e = jnp.where(counts > 0, (ends - 1) // EXPERT_ROWS - first_blk + 1, 0)
    item_end = jnp.cumsum(n_blk_e).astype(I32)
    item_start = item_end - n_blk_e
    n_items = item_end[-1]
    max_items = n_rows // EXPERT_ROWS + ne
    w = jnp.minimum(jnp.arange(max_items, dtype=I32), n_items - 1)
    owner = lambda i: jnp.minimum(jnp.sum((item_end[None, :] <= i[:, None]).astype(I32), axis=1), ne - 1)
    item_expert = owner(w)
    onehot = item_expert[:, None] == jnp.arange(ne, dtype=I32)[None, :]
    pick = lambda table: jnp.sum(jnp.where(onehot, table[None, :], 0), axis=1).astype(I32)
    item_block = pick(first_blk) + (w - pick(item_start))
    after = pick(item_end)
    item_next = jnp.where(after < n_items, owner(after), -1).astype(I32)

    dest = _dest_rows(starts, idx_t, rank_t)
    dest = jnp.transpose(dest, (0, 2, 1)).reshape(t // DEST_TOKENS, DEST_PER_TILE)
    xs = _dispatch(dest, h2p, _token_rows(d))
    item_lo, item_hi = pick(starts), pick(ends)
    in_lo = jnp.maximum(item_lo, item_block * EXPERT_ROWS) - item_block * EXPERT_ROWS
    in_hi = jnp.minimum(item_hi, (item_block + 1) * EXPERT_ROWS) - item_block * EXPERT_ROWS
    item_half = jnp.where(in_hi <= EXPERT_ROWS // 2, 1, jnp.where(in_lo >= EXPERT_ROWS // 2, 2, 0)).astype(I32)
    ys = _experts(item_expert, item_block, item_lo, item_hi, item_next, item_half,
                  n_items.reshape(1), xs, p["exp_w1"], p["exp_w3"], p["exp_w2"])
    ysh = _shared(h2, p["sh_w1"].astype(BF16), p["sh_w3"].astype(BF16), p["sh_w2"].astype(BF16))
    return dest, ys, ysh, w_t.T


def kernel(x, c, positions, ada_w, ada_b, norm1_g, w_in, conv_w, conv_b, rg_wa, rg_ba, rg_wi, rg_bi, rg_lambda, w_proj_rnn, w_proj_attn, w_out, norm2_g, router_w, router_bias, exp_w1, exp_w3, exp_w2, sh_w1, sh_w3, sh_w2, final_g):
    bsz, seq, d = x.shape
    assert ada_w.shape[0] == 1, "the fused final norm assumes a single layer"
    t = bsz * seq
    x2 = x.reshape(t, d)
    first = lambda a: a.reshape(a.shape[1:])

    half = HEAD_DIM // 2
    inv_freq = ROPE_THETA ** (-jnp.arange(half, dtype=F32) * 2.0 / HEAD_DIM)
    freq = jnp.concatenate([inv_freq, inv_freq]).reshape(1, HEAD_DIM)
    sign = jnp.concatenate([-jnp.ones((half,), F32), jnp.ones((half,), F32)]).reshape(1, HEAD_DIM)
    cos_t, sin_t = _rope_tables(positions.reshape(t, 1), freq, sign)

    c_pad = jnp.zeros((SUBLANES, d), F32).at[:bsz].set(c)
    mod = _ada_mod(c_pad, first(ada_w), ada_b.reshape(1, -1))
    mod6 = tuple(mod[:bsz, k * d:(k + 1) * d].reshape(bsz, 1, d) for k in range(6))

    p = dict(norm1_g=first(norm1_g), w_in=first(w_in), conv_w=first(conv_w), conv_b=first(conv_b),
             rg_wa=first(rg_wa), rg_ba=first(rg_ba), rg_wi=first(rg_wi), rg_bi=first(rg_bi),
             rg_lambda=first(rg_lambda), w_proj_rnn=first(w_proj_rnn), w_proj_attn=first(w_proj_attn),
             w_out=first(w_out), norm2_g=first(norm2_g), router_w=first(router_w),
             router_bias=first(router_bias), exp_w1=first(exp_w1), exp_w3=first(exp_w3),
             exp_w2=first(exp_w2), sh_w1=first(sh_w1), sh_w3=first(sh_w3), sh_w2=first(sh_w2))
    x1, h2, h2p = _mixer(x2, mod6, cos_t, sin_t, bsz, seq, p)
    dest, ys, ysh, w_tok = _moe(h2, h2p, p)
    out = _combine(dest, ys, x1, ysh, w_tok, mod6[5], final_g.reshape(1, d), seq)
    return out.reshape(bsz, seq, d)
```

```python
import functools
import math

import jax
import jax.numpy as jnp
import numpy as np
from jax import lax
from jax.experimental import pallas as pl
from jax.experimental.pallas import tpu as pltpu

F32 = jnp.float32
BF16 = jnp.bfloat16
I32 = jnp.int32

HEAD_DIM = 128
HEADS_PER_GROUP = 4
DILATION_GROUPS = ((128, 1), (512, 4), (2048, 16))
ROPE_THETA = 10000.0
CONV_WIDTH = 4
LRU_C = 8.0
TOP_K = 8
N_EXPERT_GROUPS = 8
TOPK_GROUPS = 4
ROUTED_SCALE = 2.5
NORM_EPS = 1e-6

LANES = 128
SUBLANES = 8
VMEM_LIMIT_BYTES = 56 * 1024 * 1024

GROUP_COLS = HEADS_PER_GROUP * HEAD_DIM
EXPERT_ROWS = 256
DEST_TOKENS = 128


def _params(*sem):
    return pltpu.CompilerParams(dimension_semantics=sem, vmem_limit_bytes=VMEM_LIMIT_BYTES)


def _gelu_tanh(x):
    return 0.5 * x * (1.0 + jnp.tanh(math.sqrt(2.0 / math.pi) * (x + 0.044715 * (x * x * x))))


def _silu(x):
    return x * jax.nn.sigmoid(x)


def _rms(x, g):
    ms = jnp.mean(x * x, axis=-1, keepdims=True)
    return x * lax.rsqrt(ms + NORM_EPS) * g


def _ada_kernel(c_ref, w_ref, b_ref, o_ref):
    a = _silu(c_ref[...]).astype(BF16)
    o_ref[...] = jnp.dot(a, w_ref[...].astype(BF16), preferred_element_type=F32) + b_ref[...]


def _ada_mod(c_pad, ada_w, ada_b):
    rows, d = c_pad.shape
    n = ada_w.shape[1]
    tn = 1024
    return pl.pallas_call(
        _ada_kernel,
        grid=(n // tn,),
        in_specs=[
            pl.BlockSpec((rows, d), lambda j: (0, 0)),
            pl.BlockSpec((d, tn), lambda j: (0, j)),
            pl.BlockSpec((1, tn), lambda j: (0, j)),
        ],
        out_specs=pl.BlockSpec((rows, tn), lambda j: (0, j)),
        out_shape=jax.ShapeDtypeStruct((rows, n), F32),
        compiler_params=_params("arbitrary"),
        name="ada_mod",
    )(c_pad, ada_w, ada_b)


def _rope_kernel(pos_ref, freq_ref, sign_ref, cos_ref, sin_ref):
    ang = pos_ref[...].astype(F32) * freq_ref[...]
    cos_ref[...] = jnp.cos(ang)
    sin_ref[...] = jnp.sin(ang) * sign_ref[...]


def _rope_tables(pos_col, freq, sign):
    t = pos_col.shape[0]
    tm = 1024
    return pl.pallas_call(
        _rope_kernel,
        grid=(t // tm,),
        in_specs=[
            pl.BlockSpec((tm, 1), lambda i: (i, 0)),
            pl.BlockSpec((1, HEAD_DIM), lambda i: (0, 0)),
            pl.BlockSpec((1, HEAD_DIM), lambda i: (0, 0)),
        ],
        out_specs=[pl.BlockSpec((tm, HEAD_DIM), lambda i: (i, 0))] * 2,
        out_shape=[jax.ShapeDtypeStruct((t, HEAD_DIM), F32)] * 2,
        compiler_params=_params("arbitrary"),
        name="rope_tables",
    )(pos_col, freq, sign)


def _norm_kernel(x_ref, g_ref, sh_ref, sc_ref, h_ref):
    h_ref[...] = (_rms(x_ref[...], g_ref[...]) * (1.0 + sc_ref[...]) + sh_ref[...]).astype(h_ref.dtype)


def _norm_mod(x2, g, shift, scale, seq):
    t, d = x2.shape
    tm = 1024
    per_b = seq // tm
    bvec = pl.BlockSpec((None, 1, d), lambda i: (i // per_b, 0, 0))
    return pl.pallas_call(
        _norm_kernel,
        grid=(t // tm,),
        in_specs=[pl.BlockSpec((tm, d), lambda i: (i, 0)), pl.BlockSpec((1, d), lambda i: (0, 0)), bvec, bvec],
        out_specs=pl.BlockSpec((tm, d), lambda i: (i, 0)),
        out_shape=jax.ShapeDtypeStruct((t, d), BF16),
        compiler_params=_params("arbitrary"),
        name="norm_mod",
    )(x2, g, shift, scale)


def _proj_act_kernel(h_ref, w_ref, o_ref, *, act):
    acc = jnp.dot(h_ref[...], w_ref[...], preferred_element_type=F32)
    o_ref[...] = (acc if act is None else act(acc)).astype(o_ref.dtype)


def _sigmoid_tanh(x):
    return 0.5 + 0.5 * jnp.tanh(0.5 * x)


def _proj_act(h1, w_b, act, name):
    t, d = h1.shape
    ncols = w_b.shape[1]
    tm = 1024
    tn = 1024 if ncols % 1024 == 0 else GROUP_COLS
    return pl.pallas_call(
        functools.partial(_proj_act_kernel, act=act),
        grid=(t // tm, ncols // tn),
        in_specs=[pl.BlockSpec((tm, d), lambda i, j: (i, 0)),
                  pl.BlockSpec((d, tn), lambda i, j: (0, j))],
        out_specs=pl.BlockSpec((tm, tn), lambda i, j: (i, j)),
        out_shape=jax.ShapeDtypeStruct((t, ncols), BF16),
        compiler_params=_params("arbitrary", "arbitrary"),
        name=name,
    )(h1, w_b)


def _proj_qkv_kernel(h_ref, w_ref, cos_ref, sin_ref, o_ref, scr, *, dilation):
    tm = h_ref.shape[0]
    acc = jnp.dot(h_ref[...], w_ref[...], preferred_element_type=F32)
    j = pl.program_id(1)

    @pl.when(j < 2)
    def _():
        scale = jnp.where(j == 0, HEAD_DIM ** -0.5, 1.0).astype(F32)
        c = cos_ref[...] * scale
        s = sin_ref[...] * scale
        for h in range(HEADS_PER_GROUP):
            v = acc[:, h * HEAD_DIM:(h + 1) * HEAD_DIM]
            scr[h] = v * c + pltpu.roll(v, HEAD_DIM // 2, axis=1) * s

    @pl.when(j == 2)
    def _():
        for h in range(HEADS_PER_GROUP):
            scr[h] = acc[:, h * HEAD_DIM:(h + 1) * HEAD_DIM]

    sub = tm // dilation
    for r in range(dilation):
        for h in range(HEADS_PER_GROUP):
            rows = scr[h] if dilation == 1 else scr[h, pl.ds(r, sub, stride=dilation), :]
            o_ref[r, :, h * HEAD_DIM:(h + 1) * HEAD_DIM] = rows.astype(o_ref.dtype)


def _proj_qkv(h1, w_qkv_b, cos_t, sin_t, g, dilation, bsz, seq):
    t, d = h1.shape
    tm, tn = 1024, GROUP_COLS
    per_b = seq // tm
    sub = tm // dilation
    return pl.pallas_call(
        functools.partial(_proj_qkv_kernel, dilation=dilation),
        grid=(t // tm, 3),
        in_specs=[pl.BlockSpec((tm, d), lambda i, j: (i, 0)),
                  pl.BlockSpec((d, tn), lambda i, j: (0, j)),
                  pl.BlockSpec((tm, HEAD_DIM), lambda i, j: (i, 0)),
                  pl.BlockSpec((tm, HEAD_DIM), lambda i, j: (i, 0))],
        out_specs=pl.BlockSpec((None, dilation, sub, tn), lambda i, j: (i // per_b, 0, i % per_b, j)),
        out_shape=jax.ShapeDtypeStruct((bsz, dilation, seq // dilation, 3 * tn), BF16),
        scratch_shapes=[pltpu.VMEM((HEADS_PER_GROUP, tm, HEAD_DIM), F32)],
        compiler_params=_params("arbitrary", "arbitrary"),
        name=f"proj_qkv_g{g}",
    )(h1, w_qkv_b, cos_t, sin_t)


def _rglru_kernel(xr_ref, gr_ref, cw_ref, cb_ref, wa_ref, ba_ref, wi_ref, bi_ref, lam_ref,
                  ya_ref, xbuf, a_scr, b_scr, hcar):
    tt = xr_ref.shape[0]
    nc = xbuf.shape[0]
    halo = SUBLANES
    seg = tt // SUBLANES
    pitch = a_scr.shape[1] // SUBLANES
    lanes = lambda c: slice(c * LANES, (c + 1) * LANES)

    @pl.when(pl.program_id(2) == 0)
    def _():
        for c in range(nc):
            xbuf[c, 0:halo, :] = jnp.zeros((halo, LANES), F32)
        hcar[...] = jnp.zeros_like(hcar)

    for c in range(nc):
        cs = lanes(c)
        xbuf[c, halo:halo + tt, :] = xr_ref[:, cs].astype(F32)
        z = -lam_ref[:, cs]
        softplus = jnp.maximum(z, 0.0) + jnp.log1p(jnp.exp(-jnp.abs(z)))
        quarter = (-0.25 * LRU_C) * softplus
        for s in range(SUBLANES):
            r0 = halo + s * seg
            u = cb_ref[:, cs] + cw_ref[CONV_WIDTH - 1:CONV_WIDTH, cs] * xbuf[c, r0:r0 + seg, :]
            for j in range(CONV_WIDTH - 1):
                back = CONV_WIDTH - 1 - j
                u = u + cw_ref[j:j + 1, cs] * xbuf[c, r0 - back:r0 - back + seg, :]
            ub = u.astype(BF16)
            tr = jnp.tanh(jnp.dot(ub, wa_ref[c], preferred_element_type=F32) + ba_ref[:, cs])
            ti = jnp.tanh(jnp.dot(ub, wi_ref[c], preferred_element_type=F32) + bi_ref[:, cs])
            t = jnp.tanh(quarter + quarter * tr)
            q = 1.0 / (1.0 - t)
            a_scr[c, s * pitch:s * pitch + seg, :] = (1.0 + t) * q
            b_scr[c, s * pitch:s * pitch + seg, :] = q * jnp.sqrt(-t) * (u + u * ti)
        xbuf[c, 0:halo, :] = xbuf[c, tt:tt + halo, :]

    h = [jnp.zeros((SUBLANES, LANES), F32)] * nc
    prod = [jnp.ones((SUBLANES, LANES), F32)] * nc
    for j in range(seg):
        rows_j = pl.ds(j, SUBLANES, stride=pitch)
        for c in range(nc):
            aj = a_scr[c, rows_j, :]
            h[c] = aj * h[c] + b_scr[c, rows_j, :]
            prod[c] = aj * prod[c]
            b_scr[c, rows_j, :] = h[c]
            a_scr[c, rows_j, :] = prod[c]
    for c in range(nc):
        carry = hcar[0:1, lanes(c)]
        for s in range(SUBLANES):
            rs = slice(s * seg, (s + 1) * seg)
            ps = slice(s * pitch, s * pitch + seg)
            state = b_scr[c, ps, :] + a_scr[c, ps, :] * carry
            ya_ref[rs, lanes(c)] = (state * gr_ref[rs, lanes(c)].astype(F32)).astype(ya_ref.dtype)
            carry = h[c][s:s + 1, :] + prod[c][s:s + 1, :] * carry
        hcar[:, lanes(c)] = jnp.broadcast_to(carry, (SUBLANES, LANES))


RGLRU_CHANNEL_TILES = 4


def _rglru(xr, gr, conv_w, conv_b, wa_b, ba, wi_b, bi, lam, bsz, seq, d_rnn):
    t = xr.shape[0]
    tt = 512
    nc = RGLRU_CHANNEL_TILES
    width = nc * LANES
    per_b = seq // tt
    row = pl.BlockSpec((tt, width), lambda b, c, s: (b * per_b + s, c))
    vec = pl.BlockSpec((1, width), lambda b, c, s: (0, c))
    gate_w = pl.BlockSpec((nc, LANES, LANES), lambda b, c, s: (c, 0, 0))
    scan = pltpu.VMEM((nc, tt + SUBLANES * SUBLANES, LANES), F32)
    return pl.pallas_call(
        _rglru_kernel,
        grid=(bsz, d_rnn // width, per_b),
        in_specs=[row, row, pl.BlockSpec((CONV_WIDTH, width), lambda b, c, s: (0, c)), vec,
                  gate_w, vec, gate_w, vec, vec],
        out_specs=row,
        out_shape=jax.ShapeDtypeStruct((t, d_rnn), BF16),
        scratch_shapes=[pltpu.VMEM((nc, tt + SUBLANES, LANES), F32), scan, scan, pltpu.VMEM((SUBLANES, width), F32)],
        compiler_params=_params("arbitrary", "arbitrary", "arbitrary"),
        name="rglru",
    )(xr, gr, conv_w, conv_b, wa_b, ba, wi_b, bi, lam)


ATTN_BLOCKS = 4


def _attn_kernel(q_ref, kc_ref, kp_ref, vc_ref, vp_ref, o_ref, l_ref, *, blk):
    nq = q_ref.shape[0] // blk
    not_first = pl.program_id(2) > 0
    qi = lax.broadcasted_iota(I32, (blk, blk), 0)
    kj = lax.broadcasted_iota(I32, (blk, blk), 1)
    tri_prev = kj >= qi
    mask_cur = kj <= qi
    nt = (((1,), (1,)), ((), ()))
    units = [(j, h) for j in range(nq) for h in range(HEADS_PER_GROUP)]
    rows = lambda j: slice(j * blk, (j + 1) * blk)
    cols = lambda h: slice(h * HEAD_DIM, (h + 1) * HEAD_DIM)

    def prev_kv(ref, pref, j, h):
        return pref[:, cols(h)] if j == 0 else ref[rows(j - 1), cols(h)]

    scores = []
    for j, h in units:
        q = q_ref[rows(j), cols(h)]
        sp = lax.dot_general(q, prev_kv(kc_ref, kp_ref, j, h), nt, preferred_element_type=F32)
        sc = lax.dot_general(q, kc_ref[rows(j), cols(h)], nt, preferred_element_type=F32)
        mask_prev = jnp.logical_and(tri_prev, not_first) if j == 0 else tri_prev
        scores.append((jnp.where(mask_prev, sp, -jnp.inf), jnp.where(mask_cur, sc, -jnp.inf)))
    maxes = [jnp.maximum(jnp.max(sp, axis=-1, keepdims=True), jnp.max(sc, axis=-1, keepdims=True))
             for sp, sc in scores]
    probs = [(jnp.exp(sp - m), jnp.exp(sc - m)) for (sp, sc), m in zip(scores, maxes)]
    dens = [jnp.sum(pp, axis=-1, keepdims=True) + jnp.sum(pc, axis=-1, keepdims=True) for pp, pc in probs]
    for (j, h), (pp, pc), m, den in zip(units, probs, maxes, dens):
        inv = 1.0 / den
        out = (jnp.dot((pp * inv).astype(BF16), prev_kv(vc_ref, vp_ref, j, h), preferred_element_type=F32)
               + jnp.dot((pc * inv).astype(BF16), vc_ref[rows(j), cols(h)], preferred_element_type=F32))
        o_ref[rows(j), cols(h)] = out.astype(o_ref.dtype)
        l_ref[rows(j), cols(h)] = jnp.broadcast_to(m + jnp.log(den), (blk, HEAD_DIM))


def _attention_group(qkv, g, window, dilation):
    bsz, _, length, _ = qkv.shape
    blk = window // dilation
    nb = length // blk
    nq = math.gcd(ATTN_BLOCKS, nb)
    cur = lambda c: (lambda b, r, n: (b, r, n, c))
    prev = lambda c: (lambda b, r, n: (b, r, jnp.maximum(n * nq - 1, 0), c))
    spec = lambda f: pl.BlockSpec((None, None, nq * blk, GROUP_COLS), f)
    pspec = lambda f: pl.BlockSpec((None, None, blk, GROUP_COLS), f)
    out_sds = lambda dt: jax.ShapeDtypeStruct((bsz, dilation, length, GROUP_COLS), dt)
    return pl.pallas_call(
        functools.partial(_attn_kernel, blk=blk),
        grid=(bsz, dilation, nb // nq),
        in_specs=[spec(cur(0)), spec(cur(1)), pspec(prev(1)), spec(cur(2)), pspec(prev(2))],
        out_specs=[spec(cur(0)), spec(cur(0))],
        out_shape=[out_sds(BF16), out_sds(F32)],
        compiler_params=_params("arbitrary", "arbitrary", "arbitrary"),
        name=f"attn_g{g}",
    )(qkv, qkv, qkv, qkv, qkv)


def _merge_kernel(ya_ref, o0, o1, o2, l0, l1, l2, g_ref, wr_ref, wa_ref, m_ref, yb_scr, o_scr, l_scr):
    tm = ya_ref.shape[0]
    d = m_ref.shape[1]
    for g, (o_ref, l_ref) in enumerate(((o0, l0), (o1, l1), (o2, l2))):
        dil = o_ref.shape[0]
        for r in range(dil):
            for h in range(HEADS_PER_GROUP):
                cs = slice(h * HEAD_DIM, (h + 1) * HEAD_DIM)
                if dil == 1:
                    o_scr[g, h] = o_ref[r, :, cs].astype(F32)
                    l_scr[g, h] = l_ref[r, :, cs]
                else:
                    o_scr[g, h, pl.ds(r, tm // dil, stride=dil), :] = o_ref[r, :, cs].astype(F32)
                    l_scr[g, h, pl.ds(r, tm // dil, stride=dil), :] = l_ref[r, :, cs]
    for h in range(HEADS_PER_GROUP):
        la, lb, lc = l_scr[0, h], l_scr[1, h], l_scr[2, h]
        m = jnp.maximum(jnp.maximum(la, lb), lc)
        ea, eb, ec = jnp.exp(la - m), jnp.exp(lb - m), jnp.exp(lc - m)
        inv = 1.0 / (ea + eb + ec)
        yb = (ea * inv) * o_scr[0, h] + (eb * inv) * o_scr[1, h] + (ec * inv) * o_scr[2, h]
        yb_scr[:, h * HEAD_DIM:(h + 1) * HEAD_DIM] = yb.astype(BF16)

    ya = ya_ref[...]
    yb = yb_scr[...]
    for c in range(d // GROUP_COLS):
        cs = slice(c * GROUP_COLS, (c + 1) * GROUP_COLS)
        gs = slice(d + c * GROUP_COLS, d + (c + 1) * GROUP_COLS)
        pa = jnp.dot(ya, wr_ref[:, cs], preferred_element_type=F32)
        pb = jnp.dot(yb, wa_ref[:, cs], preferred_element_type=F32)
        m_ref[:, cs] = (g_ref[:, cs].astype(F32) * pa + g_ref[:, gs].astype(F32) * pb).astype(m_ref.dtype)


def _merge(ya, outs, lses, gates, wr_b, wa_b, seq):
    t, d_rnn = ya.shape
    d = wr_b.shape[1]
    tm = 512
    per_b = seq // tm
    n_groups = len(outs)
    resident = lambda shape: pl.BlockSpec(shape, lambda i: (0, 0), pipeline_mode=pl.Buffered(1))

    def grp(o):
        dil = o.shape[1]
        return pl.BlockSpec((None, dil, tm // dil, GROUP_COLS), lambda i: (i // per_b, 0, i % per_b, 0))

    return pl.pallas_call(
        _merge_kernel,
        grid=(t // tm,),
        in_specs=[
            pl.BlockSpec((tm, d_rnn), lambda i: (i, 0)),
            *[grp(o) for o in outs], *[grp(l) for l in lses],
            pl.BlockSpec((tm, 2 * d), lambda i: (i, 0)),
            resident((d_rnn, d)), resident((GROUP_COLS, d)),
        ],
        out_specs=pl.BlockSpec((tm, d), lambda i: (i, 0)),
        out_shape=jax.ShapeDtypeStruct((t, d), BF16),
        scratch_shapes=[pltpu.VMEM((tm, GROUP_COLS), BF16),
                        pltpu.VMEM((n_groups, HEADS_PER_GROUP, tm, HEAD_DIM), F32),
                        pltpu.VMEM((n_groups, HEADS_PER_GROUP, tm, HEAD_DIM), F32)],
        compiler_params=_params("arbitrary"),
        name="merge_proj",
    )(ya, *outs, *lses, gates, wr_b, wa_b)


U32 = jnp.uint32
HIGH_HALF = np.uint32(0xFFFF0000)


def _token_rows(d):
    assert d % (2 * LANES) == 0
    return d // (2 * LANES)


def _pack_rows(v, dst_ref):
    rows, d = v.shape
    tr = _token_rows(d)
    lo = pltpu.bitcast(v[:, :d // 2].astype(BF16).astype(F32), U32)
    hi = pltpu.bitcast(v[:, d // 2:].astype(BF16).astype(F32), U32)
    word = (lo >> 16) | (hi & HIGH_HALF)
    for s in range(tr):
        dst_ref[pl.ds(s, rows, stride=tr), :] = word[:, s * LANES:(s + 1) * LANES]


def _unpack_slab(src_ref, s, rows, tr):
    word = src_ref[pl.ds(s, rows, stride=tr), :]
    return pltpu.bitcast(word << 16, F32), pltpu.bitcast(word & HIGH_HALF, F32)


def _outproj_kernel(m_ref, w_ref, x_ref, gate_ref, g2_ref, sh_ref, sc_ref, x1_ref, h2_ref, h2p_ref):
    x1 = x_ref[...] + gate_ref[...] * jnp.dot(m_ref[...], w_ref[...], preferred_element_type=F32)
    x1_ref[...] = x1
    h2 = _rms(x1, g2_ref[...]) * (1.0 + sc_ref[...]) + sh_ref[...]
    h2_ref[...] = h2.astype(h2_ref.dtype)
    _pack_rows(h2, h2p_ref)


def _out_proj(merged, w_out_b, x2, gate1, g2, shift2, scale2, seq):
    t, d = x2.shape
    tr = _token_rows(d)
    tm = 512
    per_b = seq // tm
    row = pl.BlockSpec((tm, d), lambda i: (i, 0))
    bvec = pl.BlockSpec((None, 1, d), lambda i: (i // per_b, 0, 0))
    return pl.pallas_call(
        _outproj_kernel,
        grid=(t // tm,),
        in_specs=[row, pl.BlockSpec((d, d), lambda i: (0, 0)), row, bvec,
                  pl.BlockSpec((1, d), lambda i: (0, 0)), bvec, bvec],
        out_specs=[row, row, pl.BlockSpec((tm * tr, LANES), lambda i: (i, 0))],
        out_shape=[jax.ShapeDtypeStruct((t, d), F32), jax.ShapeDtypeStruct((t, d), BF16),
                   jax.ShapeDtypeStruct((t * tr, LANES), U32)],
        compiler_params=_params("arbitrary"),
        name="out_proj",
    )(merged, w_out_b, x2, gate1, g2, shift2, scale2)


def _router_kernel(h_ref, rw_ref, bias_ref, idx_ref, w_ref, rank_ref, cnt_ref, carry):
    ne = rw_ref.shape[0]
    tm = h_ref.shape[0]
    gsz = ne // N_EXPERT_GROUPS

    @pl.when(pl.program_id(0) == 0)
    def _():
        carry[...] = jnp.zeros_like(carry)

    logits = lax.dot_general(rw_ref[...], h_ref[...].astype(BF16), (((1,), (1,)), ((), ())),
                             preferred_element_type=F32)
    scores = jax.nn.sigmoid(logits)
    sel = scores + bias_ref[...]
    row = lax.broadcasted_iota(I32, (ne, tm), 0)
    neg = -jnp.inf

    gscore = []
    rg = lax.broadcasted_iota(I32, (gsz, tm), 0)
    for g in range(N_EXPERT_GROUPS):
        sg = sel[g * gsz:(g + 1) * gsz, :]
        m1 = jnp.max(sg, axis=0, keepdims=True)
        i1 = jnp.min(jnp.where(sg == m1, rg, ne), axis=0, keepdims=True)
        m2 = jnp.max(jnp.where(rg == i1, neg, sg), axis=0, keepdims=True)
        gscore.append(m1 + m2)
    keep_rows = []
    for g in range(N_EXPERT_GROUPS):
        beaten = jnp.zeros((1, tm), I32)
        for o in range(N_EXPERT_GROUPS):
            if o == g:
                continue
            wins = (gscore[o] >= gscore[g]) if o < g else (gscore[o] > gscore[g])
            beaten = beaten + wins.astype(I32)
        keep_rows.append(jnp.broadcast_to(beaten, (gsz, tm)))
    cur = jnp.where(jnp.concatenate(keep_rows, axis=0) < TOPK_GROUPS, sel, neg)

    chosen = jnp.zeros((ne, tm), F32)
    picks, wts = [], []
    for _ in range(TOP_K):
        m = jnp.max(cur, axis=0, keepdims=True)
        ik = jnp.min(jnp.where(cur == m, row, ne), axis=0, keepdims=True)
        hit = row == ik
        wts.append(jnp.sum(jnp.where(hit, scores, 0.0), axis=0, keepdims=True))
        cur = jnp.where(hit, neg, cur)
        chosen = jnp.where(hit, 1.0, chosen)
        picks.append(ik)
    wsum = wts[0]
    for k in range(1, TOP_K):
        wsum = wsum + wts[k]

    ti = lax.broadcasted_iota(I32, (tm, tm), 0)
    tj = lax.broadcasted_iota(I32, (tm, tm), 1)
    upper = (ti < tj).astype(BF16)
    chosen_b = chosen.astype(BF16)
    before = jnp.dot(chosen_b, upper, preferred_element_type=F32)
    total = jnp.dot(chosen_b, jnp.ones((tm, LANES), BF16), preferred_element_type=F32)
    base = carry[...]
    pos = before + jnp.concatenate([base] * (tm // LANES), axis=1)
    for k in range(TOP_K):
        hit = row == picks[k]
        idx_ref[k:k + 1, :] = picks[k]
        w_ref[k:k + 1, :] = wts[k] / wsum * ROUTED_SCALE
        rank_ref[k:k + 1, :] = jnp.sum(jnp.where(hit, pos, 0.0), axis=0, keepdims=True).astype(I32)
    carry[...] = base + total
    cnt_ref[...] = base + total


def _router(h2, rw_t, bias_col):
    t, d = h2.shape
    ne = rw_t.shape[0]
    tm = 256
    kt = pl.BlockSpec((TOP_K, tm), lambda i: (0, i))
    return pl.pallas_call(
        _router_kernel,
        grid=(t // tm,),
        in_specs=[pl.BlockSpec((tm, d), lambda i: (i, 0)),
                  pl.BlockSpec((ne, d), lambda i: (0, 0)),
                  pl.BlockSpec((ne, 1), lambda i: (0, 0))],
        out_specs=[kt, kt, kt, pl.BlockSpec((ne, LANES), lambda i: (0, 0))],
        out_shape=[jax.ShapeDtypeStruct((TOP_K, t), I32), jax.ShapeDtypeStruct((TOP_K, t), F32),
                   jax.ShapeDtypeStruct((TOP_K, t), I32), jax.ShapeDtypeStruct((ne, LANES), F32)],
        scratch_shapes=[pltpu.VMEM((ne, LANES), F32)],
        compiler_params=_params("arbitrary"),
        name="router",
    )(h2, rw_t, bias_col)


def _dest_kernel(start_ref, idx_ref, rank_ref, dest_ref):
    ne = start_ref.shape[0]
    idx = idx_ref[...]

    def body(e, acc):
        return jnp.where(idx == e, start_ref[e], acc)

    dest = rank_ref[...] + lax.fori_loop(0, ne, body, jnp.zeros(idx.shape, I32))
    for j in range(dest_ref.shape[0]):
        dest_ref[j] = dest[:, j * DEST_TOKENS:(j + 1) * DEST_TOKENS]


def _dest_rows(starts, idx_t, rank_t):
    t = idx_t.shape[1]
    tb = 2048
    per = tb // DEST_TOKENS
    return pl.pallas_call(
        _dest_kernel,
        grid_spec=pltpu.PrefetchScalarGridSpec(
            num_scalar_prefetch=1,
            grid=(t // tb,),
            in_specs=[pl.BlockSpec((TOP_K, tb), lambda i, s: (0, i)),
                      pl.BlockSpec((TOP_K, tb), lambda i, s: (0, i))],
            out_specs=pl.BlockSpec((per, TOP_K, DEST_TOKENS), lambda i, s: (i, 0, 0)),
        ),
        out_shape=jax.ShapeDtypeStruct((t // DEST_TOKENS, TOP_K, DEST_TOKENS), I32),
        compiler_params=_params("arbitrary"),
        name="dest_rows",
    )(starts, idx_t, rank_t)


ROW_DMA_UNROLL = 4
DEST_PER_TILE = DEST_TOKENS * TOP_K


def _idx_copy(dest_hbm, idx_smem, isem, tile, s):
    return pltpu.make_async_copy(
        dest_hbm.at[tile], idx_smem.at[pl.ds(pl.multiple_of(s * DEST_PER_TILE, DEST_PER_TILE), DEST_PER_TILE)],
        isem.at[s])


def _token_rows_at(ref, token, tr):
    return ref.at[pl.ds(pl.multiple_of(token * tr, tr), tr), :]


def _dispatch_kernel(dest_hbm, h_ref, xs_hbm, idx_smem, isem, dsem, *, tr):
    i = pl.program_id(0)
    n = pl.num_programs(0)
    tm = h_ref.shape[0] // tr
    slot = i % 2
    idx_copy = functools.partial(_idx_copy, dest_hbm, idx_smem, isem)

    @pl.when(i == 0)
    def _():
        idx_copy(0, 0).start()

    idx_copy(i, slot).wait()

    @pl.when(i + 1 < n)
    def _():
        idx_copy(i + 1, 1 - slot).start()

    base = slot * DEST_PER_TILE

    def body(c, carry):
        for u in range(ROW_DMA_UNROLL):
            t = c * ROW_DMA_UNROLL + u
            for k in range(TOP_K):
                d = idx_smem[base + t * TOP_K + k]
                pltpu.make_async_copy(_token_rows_at(h_ref, t, tr), _token_rows_at(xs_hbm, d, tr),
                                      dsem).start(priority=k % 2)
        return carry

    lax.fori_loop(0, tm // ROW_DMA_UNROLL, body, 0)
    for k in range(TOP_K):
        pltpu.make_async_copy(h_ref, xs_hbm.at[pl.ds(0, tm * tr), :], dsem).wait()


def _dispatch(dest, h2p, tr):
    t = h2p.shape[0] // tr
    tm = DEST_TOKENS
    return pl.pallas_call(
        functools.partial(_dispatch_kernel, tr=tr),
        grid=(t // tm,),
        in_specs=[pl.BlockSpec(memory_space=pl.ANY), pl.BlockSpec((tm * tr, LANES), lambda i: (i, 0))],
        out_specs=pl.BlockSpec(memory_space=pl.ANY),
        out_shape=jax.ShapeDtypeStruct((t * TOP_K * tr, LANES), U32),
        scratch_shapes=[pltpu.SMEM((2 * DEST_PER_TILE,), I32), pltpu.SemaphoreType.DMA((2,)),
                        pltpu.SemaphoreType.DMA(())],
        compiler_params=_params("arbitrary"),
        name="dispatch",
    )(dest, h2p)


CAST_CHUNK_VREGS = 32


def _cast_rows_bf16(src_ref, dst_ref):
    n, width = src_ref.shape
    rows = max(2 * SUBLANES, CAST_CHUNK_VREGS * SUBLANES * LANES // width)
    assert n % rows == 0

    def body(i, carry):
        r = pl.ds(pl.multiple_of(i * rows, rows), rows)
        dst_ref[r, :] = src_ref[r, :].astype(BF16)
        return carry

    lax.fori_loop(0, n // rows, body, 0, unroll=2)


def _experts_kernel(e_ref, b_ref, lo_ref, hi_ref, nxt_ref, half_ref, n_ref, xs_ref, w1_hbm, w3_hbm, w2_hbm, ys_ref,
                    w1s, w3s, w2s, w1b, w3b, w2b, xb, yp, wsem):
    w = pl.program_id(0)
    prev = jnp.maximum(w - 1, 0)

    def fetch(e):
        return (pltpu.make_async_copy(w1_hbm.at[e], w1s, wsem.at[0]),
                pltpu.make_async_copy(w3_hbm.at[e], w3s, wsem.at[1]),
                pltpu.make_async_copy(w2_hbm.at[e], w2s, wsem.at[2]))

    @pl.when(w < n_ref[0])
    def _():
        @pl.when(w == 0)
        def _():
            for cp in fetch(e_ref[0]):
                cp.start()

        @pl.when(jnp.logical_or(w == 0, e_ref[w] != e_ref[prev]))
        def _():
            for cp in fetch(e_ref[w]):
                cp.wait()
            _cast_rows_bf16(w1s, w1b)
            _cast_rows_bf16(w3s, w3b)
            _cast_rows_bf16(w2s, w2b)

            @pl.when(nxt_ref[w] >= 0)
            def _():
                for cp in fetch(nxt_ref[w]):
                    cp.start()

        d = xb.shape[1]
        tr = _token_rows(d)
        new_block = jnp.logical_or(w == 0, b_ref[w] != b_ref[prev])

        @pl.when(new_block)
        def _():
            for s in range(tr):
                lo, hi = _unpack_slab(xs_ref, s, EXPERT_ROWS, tr)
                xb[:, s * LANES:(s + 1) * LANES] = lo.astype(BF16)
                xb[:, d // 2 + s * LANES:d // 2 + (s + 1) * LANES] = hi.astype(BF16)
            ys_ref[...] = jnp.zeros(ys_ref.shape, U32)

        first = (lo_ref[w] - b_ref[w] * EXPERT_ROWS) * tr
        last = (hi_ref[w] - b_ref[w] * EXPERT_ROWS) * tr

        def run(row0, nrows):
            x = xb[pl.ds(row0, nrows), :]
            h1 = jnp.dot(x, w1b[...], preferred_element_type=F32)
            h3 = jnp.dot(x, w3b[...], preferred_element_type=F32)
            act = (_silu(h1) * h3).astype(BF16)
            span = pl.ds(row0 * tr, nrows * tr)
            _pack_rows(jnp.dot(act, w2b[...], preferred_element_type=F32), yp.at[span, :])
            prow = row0 * tr + lax.broadcasted_iota(I32, (nrows * tr, 1), 0)
            mine = jnp.logical_and(prow >= first, prow < last)
            ys_ref[span, :] = jnp.where(mine, yp[span, :], ys_ref[span, :])

        half_rows = EXPERT_ROWS // 2

        @pl.when(half_ref[w] == 0)
        def _():
            run(0, EXPERT_ROWS)

        @pl.when(half_ref[w] != 0)
        def _():
            run(pl.multiple_of((half_ref[w] - 1) * half_rows, half_rows), half_rows)


def _experts(item_expert, item_block, item_lo, item_hi, item_next, item_half, n_items, xs, w1, w3, w2):
    _, d, de = w1.shape
    tr = _token_rows(d)
    rows = lambda w, e, b, lo, hi, nx, hf, n: (b[w], 0)
    hbm = pl.BlockSpec(memory_space=pl.ANY)
    return pl.pallas_call(
        _experts_kernel,
        grid_spec=pltpu.PrefetchScalarGridSpec(
            num_scalar_prefetch=7,
            grid=(item_expert.shape[0],),
            in_specs=[pl.BlockSpec((EXPERT_ROWS * tr, LANES), rows), hbm, hbm, hbm],
            out_specs=pl.BlockSpec((EXPERT_ROWS * tr, LANES), rows),
            scratch_shapes=[pltpu.VMEM((d, de), F32), pltpu.VMEM((d, de), F32), pltpu.VMEM((de, d), F32),
                            pltpu.VMEM((d, de), BF16), pltpu.VMEM((d, de), BF16), pltpu.VMEM((de, d), BF16),
                            pltpu.VMEM((EXPERT_ROWS, d), BF16), pltpu.VMEM((EXPERT_ROWS * tr, LANES), U32),
                            pltpu.SemaphoreType.DMA((3,))],
        ),
        out_shape=jax.ShapeDtypeStruct(xs.shape, U32),
        compiler_params=_params("arbitrary"),
        name="experts",
    )(item_expert, item_block, item_lo, item_hi, item_next, item_half, n_items, xs, w1, w3, w2)


def _shared_kernel(h_ref, w1_ref, w3_ref, w2_ref, y_ref):
    x = h_ref[...].astype(BF16)
    h1 = jnp.dot(x, w1_ref[...], preferred_element_type=F32)
    h3 = jnp.dot(x, w3_ref[...], preferred_element_type=F32)
    y_ref[...] = jnp.dot((_silu(h1) * h3).astype(BF16), w2_ref[...], preferred_element_type=F32)


def _shared(h2, w1_b, w3_b, w2_b):
    t, d = h2.shape
    de = w1_b.shape[1]
    tm = 512
    row = pl.BlockSpec((tm, d), lambda i: (i, 0))
    return pl.pallas_call(
        _shared_kernel,
        grid=(t // tm,),
        in_specs=[row, pl.BlockSpec((d, de), lambda i: (0, 0)), pl.BlockSpec((d, de), lambda i: (0, 0)),
                  pl.BlockSpec((de, d), lambda i: (0, 0))],
        out_specs=row,
        out_shape=jax.ShapeDtypeStruct((t, d), F32),
        compiler_params=_params("arbitrary"),
        name="shared_expert",
    )(h2, w1_b, w3_b, w2_b)


def _combine_kernel(dest_hbm, ys_hbm, x1_ref, ysh_ref, wt_ref, gate_ref, fg_ref, o_ref,
                    rows0, rows1, idx_smem, isem, gsem):
    i = pl.program_id(0)
    n = pl.num_programs(0)
    tm, d = x1_ref.shape
    tr = _token_rows(d)
    slot = i % 2
    idx_copy = functools.partial(_idx_copy, dest_hbm, idx_smem, isem)

    def gather(buf, s, t, k):
        src = idx_smem[s * DEST_PER_TILE + t * TOP_K + k]
        return pltpu.make_async_copy(_token_rows_at(ys_hbm, src, tr), buf.at[k, pl.ds(t * tr, tr), :], gsem.at[s])

    def wait_gathers(buf, s):
        for k in range(TOP_K):
            pltpu.make_async_copy(ys_hbm.at[pl.ds(0, tm * tr), :], buf.at[k], gsem.at[s]).wait()

    @pl.when(i == 0)
    def _():
        idx_copy(0, 0).start()
        idx_copy(0, 0).wait()

        def body(c, carry):
            for u in range(ROW_DMA_UNROLL):
                for k in range(TOP_K):
                    gather(rows0, 0, c * ROW_DMA_UNROLL + u, k).start(priority=k % 2)
            return carry

        lax.fori_loop(0, tm // ROW_DMA_UNROLL, body, 0)

        @pl.when(n > 1)
        def _():
            idx_copy(1, 1).start()

    @pl.when(i + 1 < n)
    def _():
        idx_copy(i + 1, 1 - slot).wait()

    @pl.when(i + 2 < n)
    def _():
        idx_copy(i + 2, slot).start()

    def step(cur, cur_s, nxt, nxt_s):
        wait_gathers(cur, cur_s)
        w = wt_ref[...]
        ssq = jnp.zeros((tm, 1), F32)
        for s in range(tr):
            for t in range(s * tm // tr, (s + 1) * tm // tr):
                for k in range(TOP_K):
                    gather(nxt, nxt_s, t, k).start(priority=k % 2)
            cols = (slice(s * LANES, (s + 1) * LANES), slice(d // 2 + s * LANES, d // 2 + (s + 1) * LANES))
            acc = [ysh_ref[:, cs] for cs in cols]
            for k in range(TOP_K):
                halves = _unpack_slab(cur.at[k], s, tm, tr)
                acc = [a + v * w[:, k:k + 1] for a, v in zip(acc, halves)]
            for cs, a in zip(cols, acc):
                x2 = x1_ref[:, cs] + gate_ref[:, cs] * a
                o_ref[:, cs] = x2
                ssq = ssq + jnp.sum(x2 * x2, axis=-1, keepdims=True)
        o_ref[...] = o_ref[...] * lax.rsqrt(ssq * (1.0 / d) + NORM_EPS) * fg_ref[...]

        @pl.when(i + 1 == n)
        def _():
            wait_gathers(nxt, nxt_s)

    @pl.when(slot == 0)
    def _():
        step(rows0, 0, rows1, 1)

    @pl.when(slot == 1)
    def _():
        step(rows1, 1, rows0, 0)


def _combine(dest, ys, x1, ysh, w_tok, gate2, final_g, seq):
    t, d = x1.shape
    tr = _token_rows(d)
    tm = DEST_TOKENS
    assert t // tm >= 2, "the gather ring keeps two token tiles in flight"
    per_b = seq // tm
    row = pl.BlockSpec((tm, d), lambda i: (i, 0))
    return pl.pallas_call(
        _combine_kernel,
        grid=(t // tm,),
        in_specs=[pl.BlockSpec(memory_space=pl.ANY), pl.BlockSpec(memory_space=pl.ANY), row, row,
                  pl.BlockSpec((tm, TOP_K), lambda i: (i, 0)),
                  pl.BlockSpec((None, 1, d), lambda i: (i // per_b, 0, 0)),
                  pl.BlockSpec((1, d), lambda i: (0, 0))],
        out_specs=row,
        out_shape=jax.ShapeDtypeStruct((t, d), F32),
        scratch_shapes=[pltpu.VMEM((TOP_K, tm * tr, LANES), U32), pltpu.VMEM((TOP_K, tm * tr, LANES), U32),
                        pltpu.SMEM((2 * DEST_PER_TILE,), I32),
                        pltpu.SemaphoreType.DMA((2,)), pltpu.SemaphoreType.DMA((2,))],
        compiler_params=_params("arbitrary"),
        name="combine",
    )(dest, ys, x1, ysh, w_tok, gate2, final_g)


def _mixer(x2, mod6, cos_t, sin_t, bsz, seq, p):
    t, d = x2.shape
    shift1, scale1, gate1, shift2, scale2, _ = mod6
    d_rnn = p["conv_w"].shape[1]
    att_width = len(DILATION_GROUPS) * GROUP_COLS
    q_col = 2 * d_rnn
    gate_col = 2 * d_rnn + 3 * att_width

    def qkv_weights(g):
        parts = [p["w_in"][:, q_col + j * att_width + g * GROUP_COLS:q_col + j * att_width + (g + 1) * GROUP_COLS]
                 for j in range(3)]
        return jnp.concatenate(parts, axis=1).astype(BF16)

    h1 = _norm_mod(x2, p["norm1_g"].reshape(1, d), shift1, scale1, seq)
    xr = _proj_act(h1, p["w_in"][:, :d_rnn].astype(BF16), None, "proj_rnn_x")
    gr = _proj_act(h1, p["w_in"][:, d_rnn:2 * d_rnn].astype(BF16), _gelu_tanh, "proj_rnn_gate")
    gates = _proj_act(h1, p["w_in"][:, gate_col:].astype(BF16), _sigmoid_tanh, "proj_gates")
    ya = _rglru(xr, gr, p["conv_w"], p["conv_b"].reshape(1, d_rnn),
                (0.5 * p["rg_wa"]).astype(BF16), 0.5 * p["rg_ba"].reshape(1, d_rnn),
                (0.5 * p["rg_wi"]).astype(BF16), 0.5 * p["rg_bi"].reshape(1, d_rnn),
                p["rg_lambda"].reshape(1, d_rnn), bsz, seq, d_rnn)

    outs, lses = [], []
    for g, (window, dilation) in enumerate(DILATION_GROUPS):
        qkv = _proj_qkv(h1, qkv_weights(g), cos_t, sin_t, g, dilation, bsz, seq)
        o, l = _attention_group(qkv, g, window, dilation)
        outs.append(o)
        lses.append(l)

    merged = _merge(ya, outs, lses, gates, p["w_proj_rnn"].astype(BF16), p["w_proj_attn"].astype(BF16), seq)
    return _out_proj(merged, p["w_out"].astype(BF16), x2, gate1, p["norm2_g"].reshape(1, d),
                     shift2, scale2, seq)


def _moe(h2, h2p, p):
    t, d = h2.shape
    ne = p["router_w"].shape[1]
    idx_t, w_t, rank_t, cnt = _router(h2, p["router_w"].T.astype(BF16), p["router_bias"].reshape(ne, 1))

    counts = cnt[:, 0].astype(I32)
    ends = jnp.cumsum(counts).astype(I32)
    starts = ends - counts
    n_rows = t * TOP_K
    first_blk = starts // EXPERT_ROWS
    n_blk_e = jnp.where(counts > 0, (ends - 1) // EXPERT_ROWS - first_blk + 1, 0)
    item_end = jnp.cumsum(n_blk_e).astype(I32)
    item_start = item_end - n_blk_e
    n_items = item_end[-1]
    max_items = n_rows // EXPERT_ROWS + ne
    w = jnp.minimum(jnp.arange(max_items, dtype=I32), n_items - 1)
    owner = lambda i: jnp.minimum(jnp.sum((item_end[None, :] <= i[:, None]).astype(I32), axis=1), ne - 1)
    item_expert = owner(w)
    onehot = item_expert[:, None] == jnp.arange(ne, dtype=I32)[None, :]
    pick = lambda table: jnp.sum(jnp.where(onehot, table[None, :], 0), axis=1).astype(I32)
    item_block = pick(first_blk) + (w - pick(item_start))
    after = pick(item_end)
    item_next = jnp.where(after < n_items, owner(after), -1).astype(I32)

    dest = _dest_rows(starts, idx_t, rank_t)
    dest = jnp.transpose(dest, (0, 2, 1)).reshape(t // DEST_TOKENS, DEST_PER_TILE)
    xs = _dispatch(dest, h2p, _token_rows(d))
    item_lo, item_hi = pick(starts), pick(ends)
    in_lo = jnp.maximum(item_lo, item_block * EXPERT_ROWS) - item_block * EXPERT_ROWS
    in_hi = jnp.minimum(item_hi, (item_block + 1) * EXPERT_ROWS) - item_block * EXPERT_ROWS
    item_half = jnp.where(in_hi <= EXPERT_ROWS // 2, 1, jnp.where(in_lo >= EXPERT_ROWS // 2, 2, 0)).astype(I32)
    ys = _experts(item_expert, item_block, item_lo, item_hi, item_next, item_half,
                  n_items.reshape(1), xs, p["exp_w1"], p["exp_w3"], p["exp_w2"])
    ysh = _shared(h2, p["sh_w1"].astype(BF16), p["sh_w3"].astype(BF16), p["sh_w2"].astype(BF16))
    return dest, ys, ysh, w_t.T


def kernel(x, c, positions, ada_w, ada_b, norm1_g, w_in, conv_w, conv_b, rg_wa, rg_ba, rg_wi, rg_bi, rg_lambda, w_proj_rnn, w_proj_attn, w_out, norm2_g, router_w, router_bias, exp_w1, exp_w3, exp_w2, sh_w1, sh_w3, sh_w2, final_g):
    bsz, seq, d = x.shape
    assert ada_w.shape[0] == 1, "the fused final norm assumes a single layer"
    t = bsz * seq
    x2 = x.reshape(t, d)
    first = lambda a: a.reshape(a.shape[1:])

    half = HEAD_DIM // 2
    inv_freq = ROPE_THETA ** (-jnp.arange(half, dtype=F32) * 2.0 / HEAD_DIM)
    freq = jnp.concatenate([inv_freq, inv_freq]).reshape(1, HEAD_DIM)
    sign = jnp.concatenate([-jnp.ones((half,), F32), jnp.ones((half,), F32)]).reshape(1, HEAD_DIM)
    cos_t, sin_t = _rope_tables(positions.reshape(t, 1), freq, sign)

    c_pad = jnp.zeros((SUBLANES, d), F32).at[:bsz].set(c)
    mod = _ada_mod(c_pad, first(ada_w), ada_b.reshape(1, -1))
    mod6 = tuple(mod[:bsz, k * d:(k + 1) * d].reshape(bsz, 1, d) for k in range(6))

    p = dict(norm1_g=first(norm1_g), w_in=first(w_in), conv_w=first(conv_w), conv_b=first(conv_b),
             rg_wa=first(rg_wa), rg_ba=first(rg_ba), rg_wi=first(rg_wi), rg_bi=first(rg_bi),
             rg_lambda=first(rg_lambda), w_proj_rnn=first(w_proj_rnn), w_proj_attn=first(w_proj_attn),
             w_out=first(w_out), norm2_g=first(norm2_g), router_w=first(router_w),
             router_bias=first(router_bias), exp_w1=first(exp_w1), exp_w3=first(exp_w3),
             exp_w2=first(exp_w2), sh_w1=first(sh_w1), sh_w3=first(sh_w3), sh_w2=first(sh_w2))
    x1, h2, h2p = _mixer(x2, mod6, cos_t, sin_t, bsz, seq, p)
    dest, ys, ysh, w_tok = _moe(h2, h2p, p)
    out = _combine(dest, ys, x1, ysh, w_tok, mod6[5], final_g.reshape(1, d), seq)
    return out.reshape(bsz, seq, d)
```

```python
import functools
import math

import jax
import jax.numpy as jnp
import numpy as np
from jax import lax
from jax.experimental import pallas as pl
from jax.experimental.pallas import tpu as pltpu

F32 = jnp.float32
BF16 = jnp.bfloat16
I32 = jnp.int32

HEAD_DIM = 128
HEADS_PER_GROUP = 4
DILATION_GROUPS = ((128, 1), (512, 4), (2048, 16))
ROPE_THETA = 10000.0
CONV_WIDTH = 4
LRU_C = 8.0
TOP_K = 8
N_EXPERT_GROUPS = 8
TOPK_GROUPS = 4
ROUTED_SCALE = 2.5
NORM_EPS = 1e-6

LANES = 128
SUBLANES = 8
VMEM_LIMIT_BYTES = 56 * 1024 * 1024

GROUP_COLS = HEADS_PER_GROUP * HEAD_DIM
EXPERT_ROWS = 256
DEST_TOKENS = 128


def _params(*sem):
    return pltpu.CompilerParams(dimension_semantics=sem, vmem_limit_bytes=VMEM_LIMIT_BYTES)


def _gelu_tanh(x):
    return 0.5 * x * (1.0 + jnp.tanh(math.sqrt(2.0 / math.pi) * (x + 0.044715 * (x * x * x))))


def _silu(x):
    return x * jax.nn.sigmoid(x)


def _rms(x, g):
    ms = jnp.mean(x * x, axis=-1, keepdims=True)
    return x * lax.rsqrt(ms + NORM_EPS) * g


def _ada_kernel(c_ref, w_ref, b_ref, o_ref):
    a = _silu(c_ref[...]).astype(BF16)
    o_ref[...] = jnp.dot(a, w_ref[...].astype(BF16), preferred_element_type=F32) + b_ref[...]


def _ada_mod(c_pad, ada_w, ada_b):
    rows, d = c_pad.shape
    n = ada_w.shape[1]
    tn = 1024
    return pl.pallas_call(
        _ada_kernel,
        grid=(n // tn,),
        in_specs=[
            pl.BlockSpec((rows, d), lambda j: (0, 0)),
            pl.BlockSpec((d, tn), lambda j: (0, j)),
            pl.BlockSpec((1, tn), lambda j: (0, j)),
        ],
        out_specs=pl.BlockSpec((rows, tn), lambda j: (0, j)),
        out_shape=jax.ShapeDtypeStruct((rows, n), F32),
        compiler_params=_params("arbitrary"),
        name="ada_mod",
    )(c_pad, ada_w, ada_b)


def _rope_kernel(pos_ref, freq_ref, sign_ref, cos_ref, sin_ref):
    ang = pos_ref[...].astype(F32) * freq_ref[...]
    cos_ref[...] = jnp.cos(ang)
    sin_ref[...] = jnp.sin(ang) * sign_ref[...]


def _rope_tables(pos_col, freq, sign):
    t = pos_col.shape[0]
    tm = 1024
    return pl.pallas_call(
        _rope_kernel,
        grid=(t // tm,),
        in_specs=[
            pl.BlockSpec((tm, 1), lambda i: (i, 0)),
            pl.BlockSpec((1, HEAD_DIM), lambda i: (0, 0)),
            pl.BlockSpec((1, HEAD_DIM), lambda i: (0, 0)),
        ],
        out_specs=[pl.BlockSpec((tm, HEAD_DIM), lambda i: (i, 0))] * 2,
        out_shape=[jax.ShapeDtypeStruct((t, HEAD_DIM), F32)] * 2,
        compiler_params=_params("arbitrary"),
        name="rope_tables",
    )(pos_col, freq, sign)


def _norm_kernel(x_ref, g_ref, sh_ref, sc_ref, h_ref):
    h_ref[...] = (_rms(x_ref[...], g_ref[...]) * (1.0 + sc_ref[...]) + sh_ref[...]).astype(h_ref.dtype)


def _norm_mod(x2, g, shift, scale, seq):
    t, d = x2.shape
    tm = 1024
    per_b = seq // tm
    bvec = pl.BlockSpec((None, 1, d), lambda i: (i // per_b, 0, 0))
    return pl.pallas_call(
        _norm_kernel,
        grid=(t // tm,),
        in_specs=[pl.BlockSpec((tm, d), lambda i: (i, 0)), pl.BlockSpec((1, d), lambda i: (0, 0)), bvec, bvec],
        out_specs=pl.BlockSpec((tm, d), lambda i: (i, 0)),
        out_shape=jax.ShapeDtypeStruct((t, d), BF16),
        compiler_params=_params("arbitrary"),
        name="norm_mod",
    )(x2, g, shift, scale)


def _proj_act_kernel(h_ref, w_ref, o_ref, *, act):
    acc = jnp.dot(h_ref[...], w_ref[...], preferred_element_type=F32)
    o_ref[...] = (acc if act is None else act(acc)).astype(o_ref.dtype)


def _sigmoid_tanh(x):
    return 0.5 + 0.5 * jnp.tanh(0.5 * x)


def _proj_act(h1, w_b, act, name):
    t, d = h1.shape
    ncols = w_b.shape[1]
    tm = 1024
    tn = 1024 if ncols % 1024 == 0 else GROUP_COLS
    return pl.pallas_call(
        functools.partial(_proj_act_kernel, act=act),
        grid=(t // tm, ncols // tn),
        in_specs=[pl.BlockSpec((tm, d), lambda i, j: (i, 0)),
                  pl.BlockSpec((d, tn), lambda i, j: (0, j))],
        out_specs=pl.BlockSpec((tm, tn), lambda i, j: (i, j)),
        out_shape=jax.ShapeDtypeStruct((t, ncols), BF16),
        compiler_params=_params("arbitrary", "arbitrary"),
        name=name,
    )(h1, w_b)


def _proj_qkv_kernel(h_ref, w_ref, cos_ref, sin_ref, o_ref, scr, *, dilation):
    tm = h_ref.shape[0]
    acc = jnp.dot(h_ref[...], w_ref[...], preferred_element_type=F32)
    j = pl.program_id(1)
    is_v = j == 2
    scale = jnp.where(j == 0, HEAD_DIM ** -0.5, 1.0).astype(F32)
    c = jnp.where(is_v, 1.0, cos_ref[...] * scale)
    s = jnp.where(is_v, 0.0, sin_ref[...] * scale)
    sub = tm // dilation
    for h in range(HEADS_PER_GROUP):
        hs = slice(h * HEAD_DIM, (h + 1) * HEAD_DIM)
        v = acc[:, hs]
        res = v * c + pltpu.roll(v, HEAD_DIM // 2, axis=1) * s
        if dilation == 1:
            o_ref[0, :, hs] = res.astype(o_ref.dtype)
        else:
            scr[h] = res
    if dilation > 1:
        for r in range(dilation):
            for h in range(HEADS_PER_GROUP):
                rows = scr[h, pl.ds(r, sub, stride=dilation), :]
                o_ref[r, :, h * HEAD_DIM:(h + 1) * HEAD_DIM] = rows.astype(o_ref.dtype)


def _proj_qkv(h1, w_qkv_b, cos_t, sin_t, g, dilation, bsz, seq):
    t, d = h1.shape
    tm, tn = 1024, GROUP_COLS
    per_b = seq // tm
    sub = tm // dilation
    return pl.pallas_call(
        functools.partial(_proj_qkv_kernel, dilation=dilation),
        grid=(t // tm, 3),
        in_specs=[pl.BlockSpec((tm, d), lambda i, j: (i, 0)),
                  pl.BlockSpec((d, tn), lambda i, j: (0, j)),
                  pl.BlockSpec((tm, HEAD_DIM), lambda i, j: (i, 0)),
                  pl.BlockSpec((tm, HEAD_DIM), lambda i, j: (i, 0))],
        out_specs=pl.BlockSpec((None, dilation, sub, tn), lambda i, j: (i // per_b, 0, i % per_b, j)),
        out_shape=jax.ShapeDtypeStruct((bsz, dilation, seq // dilation, 3 * tn), BF16),
        scratch_shapes=[pltpu.VMEM((HEADS_PER_GROUP, tm, HEAD_DIM), F32)],
        compiler_params=_params("arbitrary", "arbitrary"),
        name=f"proj_qkv_g{g}",
    )(h1, w_qkv_b, cos_t, sin_t)


def _rglru_kernel(xr_ref, gr_ref, cw_ref, cb_ref, wa_ref, ba_ref, wi_ref, bi_ref, lam_ref,
                  ya_ref, xbuf, a_scr, b_scr, hcar):
    tt = xr_ref.shape[0]
    nc = xbuf.shape[0]
    halo = SUBLANES
    seg = tt // SUBLANES
    pitch = a_scr.shape[1] // SUBLANES
    lanes = lambda c: slice(c * LANES, (c + 1) * LANES)

    @pl.when(pl.program_id(2) == 0)
    def _():
        for c in range(nc):
            xbuf[c, 0:halo, :] = jnp.zeros((halo, LANES), F32)
        hcar[...] = jnp.zeros_like(hcar)

    for c in range(nc):
        cs = lanes(c)
        xbuf[c, halo:halo + tt, :] = xr_ref[:, cs].astype(F32)
        z = -lam_ref[:, cs]
        softplus = jnp.maximum(z, 0.0) + jnp.log1p(jnp.exp(-jnp.abs(z)))
        quarter = (-0.25 * LRU_C) * softplus
        for s in range(SUBLANES):
            r0 = halo + s * seg
            u = cb_ref[:, cs] + cw_ref[CONV_WIDTH - 1:CONV_WIDTH, cs] * xbuf[c, r0:r0 + seg, :]
            for j in range(CONV_WIDTH - 1):
                back = CONV_WIDTH - 1 - j
                u = u + cw_ref[j:j + 1, cs] * xbuf[c, r0 - back:r0 - back + seg, :]
            ub = u.astype(BF16)
            tr = jnp.tanh(jnp.dot(ub, wa_ref[c], preferred_element_type=F32) + ba_ref[:, cs])
            ti = jnp.tanh(jnp.dot(ub, wi_ref[c], preferred_element_type=F32) + bi_ref[:, cs])
            t = jnp.tanh(quarter + quarter * tr)
            q = 1.0 / (1.0 - t)
            a_scr[c, s * pitch:s * pitch + seg, :] = (1.0 + t) * q
            b_scr[c, s * pitch:s * pitch + seg, :] = q * jnp.sqrt(-t) * (u + u * ti)
        xbuf[c, 0:halo, :] = xbuf[c, tt:tt + halo, :]

    h = [jnp.zeros((SUBLANES, LANES), F32)] * nc
    prod = [jnp.ones((SUBLANES, LANES), F32)] * nc
    for j in range(seg):
        rows_j = pl.ds(j, SUBLANES, stride=pitch)
        for c in range(nc):
            aj = a_scr[c, rows_j, :]
            h[c] = aj * h[c] + b_scr[c, rows_j, :]
            prod[c] = aj * prod[c]
            b_scr[c, rows_j, :] = h[c]
            a_scr[c, rows_j, :] = prod[c]
    for c in range(nc):
        carry = hcar[0:1, lanes(c)]
        for s in range(SUBLANES):
            rs = slice(s * seg, (s + 1) * seg)
            ps = slice(s * pitch, s * pitch + seg)
            state = b_scr[c, ps, :] + a_scr[c, ps, :] * carry
            ya_ref[rs, lanes(c)] = (state * gr_ref[rs, lanes(c)].astype(F32)).astype(ya_ref.dtype)
            carry = h[c][s:s + 1, :] + prod[c][s:s + 1, :] * carry
        hcar[:, lanes(c)] = jnp.broadcast_to(carry, (SUBLANES, LANES))


RGLRU_CHANNEL_TILES = 4


def _rglru(xr, gr, conv_w, conv_b, wa_b, ba, wi_b, bi, lam, bsz, seq, d_rnn):
    t = xr.shape[0]
    tt = 512
    nc = RGLRU_CHANNEL_TILES
    width = nc * LANES
    per_b = seq // tt
    row = pl.BlockSpec((tt, width), lambda b, c, s: (b * per_b + s, c))
    vec = pl.BlockSpec((1, width), lambda b, c, s: (0, c))
    gate_w = pl.BlockSpec((nc, LANES, LANES), lambda b, c, s: (c, 0, 0))
    scan = pltpu.VMEM((nc, tt + SUBLANES * SUBLANES, LANES), F32)
    return pl.pallas_call(
        _rglru_kernel,
        grid=(bsz, d_rnn // width, per_b),
        in_specs=[row, row, pl.BlockSpec((CONV_WIDTH, width), lambda b, c, s: (0, c)), vec,
                  gate_w, vec, gate_w, vec, vec],
        out_specs=row,
        out_shape=jax.ShapeDtypeStruct((t, d_rnn), BF16),
        scratch_shapes=[pltpu.VMEM((nc, tt + SUBLANES, LANES), F32), scan, scan, pltpu.VMEM((SUBLANES, width), F32)],
        compiler_params=_params("arbitrary", "arbitrary", "arbitrary"),
        name="rglru",
    )(xr, gr, conv_w, conv_b, wa_b, ba, wi_b, bi, lam)


ATTN_BLOCKS = 4


def _attn_kernel(q_ref, kc_ref, kp_ref, vc_ref, vp_ref, o_ref, l_ref, *, blk):
    nq = q_ref.shape[0] // blk
    not_first = pl.program_id(2) > 0
    qi = lax.broadcasted_iota(I32, (blk, blk), 0)
    kj = lax.broadcasted_iota(I32, (blk, blk), 1)
    tri_prev = kj >= qi
    mask_cur = kj <= qi
    nt = (((1,), (1,)), ((), ()))
    units = [(j, h) for j in range(nq) for h in range(HEADS_PER_GROUP)]
    rows = lambda j: slice(j * blk, (j + 1) * blk)
    cols = lambda h: slice(h * HEAD_DIM, (h + 1) * HEAD_DIM)

    def prev_kv(ref, pref, j, h):
        return pref[:, cols(h)] if j == 0 else ref[rows(j - 1), cols(h)]

    scores = []
    for j, h in units:
        q = q_ref[rows(j), cols(h)]
        sp = lax.dot_general(q, prev_kv(kc_ref, kp_ref, j, h), nt, preferred_element_type=F32)
        sc = lax.dot_general(q, kc_ref[rows(j), cols(h)], nt, preferred_element_type=F32)
        mask_prev = jnp.logical_and(tri_prev, not_first) if j == 0 else tri_prev
        scores.append((jnp.where(mask_prev, sp, -jnp.inf), jnp.where(mask_cur, sc, -jnp.inf)))
    maxes = [jnp.maximum(jnp.max(sp, axis=-1, keepdims=True), jnp.max(sc, axis=-1, keepdims=True))
             for sp, sc in scores]
    probs = [(jnp.exp(sp - m), jnp.exp(sc - m)) for (sp, sc), m in zip(scores, maxes)]
    dens = [jnp.sum(pp, axis=-1, keepdims=True) + jnp.sum(pc, axis=-1, keepdims=True) for pp, pc in probs]
    for (j, h), (pp, pc), m, den in zip(units, probs, maxes, dens):
        inv = 1.0 / den
        out = (jnp.dot((pp * inv).astype(BF16), prev_kv(vc_ref, vp_ref, j, h), preferred_element_type=F32)
               + jnp.dot((pc * inv).astype(BF16), vc_ref[rows(j), cols(h)], preferred_element_type=F32))
        o_ref[rows(j), cols(h)] = out.astype(o_ref.dtype)
        l_ref[rows(j), cols(h)] = jnp.broadcast_to(m + jnp.log(den), (blk, HEAD_DIM))


def _attention_group(qkv, g, window, dilation):
    bsz, _, length, _ = qkv.shape
    blk = window // dilation
    nb = length // blk
    nq = math.gcd(ATTN_BLOCKS, nb)
    cur = lambda c: (lambda b, r, n: (b, r, n, c))
    prev = lambda c: (lambda b, r, n: (b, r, jnp.maximum(n * nq - 1, 0), c))
    spec = lambda f: pl.BlockSpec((None, None, nq * blk, GROUP_COLS), f)
    pspec = lambda f: pl.BlockSpec((None, None, blk, GROUP_COLS), f)
    out_sds = lambda dt: jax.ShapeDtypeStruct((bsz, dilation, length, GROUP_COLS), dt)
    return pl.pallas_call(
        functools.partial(_attn_kernel, blk=blk),
        grid=(bsz, dilation, nb // nq),
        in_specs=[spec(cur(0)), spec(cur(1)), pspec(prev(1)), spec(cur(2)), pspec(prev(2))],
        out_specs=[spec(cur(0)), spec(cur(0))],
        out_shape=[out_sds(BF16), out_sds(F32)],
        compiler_params=_params("arbitrary", "arbitrary", "arbitrary"),
        name=f"attn_g{g}",
    )(qkv, qkv, qkv, qkv, qkv)


def _merge_kernel(ya_ref, o0, o1, o2, l0, l1, l2, g_ref, wr_ref, wa_ref, m_ref, yb_scr, o_scr, l_scr):
    tm = ya_ref.shape[0]
    d = m_ref.shape[1]
    for g, (o_ref, l_ref) in enumerate(((o0, l0), (o1, l1), (o2, l2))):
        dil = o_ref.shape[0]
        for r in range(dil):
            for h in range(HEADS_PER_GROUP):
                cs = slice(h * HEAD_DIM, (h + 1) * HEAD_DIM)
                if dil == 1:
                    o_scr[g, h] = o_ref[r, :, cs].astype(F32)
                    l_scr[g, h] = l_ref[r, :, cs]
                else:
                    o_scr[g, h, pl.ds(r, tm // dil, stride=dil), :] = o_ref[r, :, cs].astype(F32)
                    l_scr[g, h, pl.ds(r, tm // dil, stride=dil), :] = l_ref[r, :, cs]
    for h in range(HEADS_PER_GROUP):
        la, lb, lc = l_scr[0, h], l_scr[1, h], l_scr[2, h]
        m = jnp.maximum(jnp.maximum(la, lb), lc)
        ea, eb, ec = jnp.exp(la - m), jnp.exp(lb - m), jnp.exp(lc - m)
        inv = 1.0 / (ea + eb + ec)
        yb = (ea * inv) * o_scr[0, h] + (eb * inv) * o_scr[1, h] + (ec * inv) * o_scr[2, h]
        yb_scr[:, h * HEAD_DIM:(h + 1) * HEAD_DIM] = yb.astype(BF16)

    ya = ya_ref[...]
    yb = yb_scr[...]
    for c in range(d // GROUP_COLS):
        cs = slice(c * GROUP_COLS, (c + 1) * GROUP_COLS)
        gs = slice(d + c * GROUP_COLS, d + (c + 1) * GROUP_COLS)
        pa = jnp.dot(ya, wr_ref[:, cs], preferred_element_type=F32)
        pb = jnp.dot(yb, wa_ref[:, cs], preferred_element_type=F32)
        m_ref[:, cs] = (g_ref[:, cs].astype(F32) * pa + g_ref[:, gs].astype(F32) * pb).astype(m_ref.dtype)


def _merge(ya, outs, lses, gates, wr_b, wa_b, seq):
    t, d_rnn = ya.shape
    d = wr_b.shape[1]
    tm = 512
    per_b = seq // tm
    n_groups = len(outs)
    resident = lambda shape: pl.BlockSpec(shape, lambda i: (0, 0), pipeline_mode=pl.Buffered(1))

    def grp(o):
        dil = o.shape[1]
        return pl.BlockSpec((None, dil, tm // dil, GROUP_COLS), lambda i: (i // per_b, 0, i % per_b, 0))

    return pl.pallas_call(
        _merge_kernel,
        grid=(t // tm,),
        in_specs=[
            pl.BlockSpec((tm, d_rnn), lambda i: (i, 0)),
            *[grp(o) for o in outs], *[grp(l) for l in lses],
            pl.BlockSpec((tm, 2 * d), lambda i: (i, 0)),
            resident((d_rnn, d)), resident((GROUP_COLS, d)),
        ],
        out_specs=pl.BlockSpec((tm, d), lambda i: (i, 0)),
        out_shape=jax.ShapeDtypeStruct((t, d), BF16),
        scratch_shapes=[pltpu.VMEM((tm, GROUP_COLS), BF16),
                        pltpu.VMEM((n_groups, HEADS_PER_GROUP, tm, HEAD_DIM), F32),
                        pltpu.VMEM((n_groups, HEADS_PER_GROUP, tm, HEAD_DIM), F32)],
        compiler_params=_params("arbitrary"),
        name="merge_proj",
    )(ya, *outs, *lses, gates, wr_b, wa_b)


U32 = jnp.uint32
HIGH_HALF = np.uint32(0xFFFF0000)


def _token_rows(d):
    assert d % (2 * LANES) == 0
    return d // (2 * LANES)


def _pack_rows(v, dst_ref):
    rows, d = v.shape
    tr = _token_rows(d)
    lo = pltpu.bitcast(v[:, :d // 2].astype(BF16).astype(F32), U32)
    hi = pltpu.bitcast(v[:, d // 2:].astype(BF16).astype(F32), U32)
    word = (lo >> 16) | (hi & HIGH_HALF)
    for s in range(tr):
        dst_ref[pl.ds(s, rows, stride=tr), :] = word[:, s * LANES:(s + 1) * LANES]


def _unpack_slab(src_ref, s, rows, tr):
    word = src_ref[pl.ds(s, rows, stride=tr), :]
    return pltpu.bitcast(word << 16, F32), pltpu.bitcast(word & HIGH_HALF, F32)


def _outproj_kernel(m_ref, w_ref, x_ref, gate_ref, g2_ref, sh_ref, sc_ref, x1_ref, h2_ref, h2p_ref):
    x1 = x_ref[...] + gate_ref[...] * jnp.dot(m_ref[...], w_ref[...], preferred_element_type=F32)
    x1_ref[...] = x1
    h2 = _rms(x1, g2_ref[...]) * (1.0 + sc_ref[...]) + sh_ref[...]
    h2_ref[...] = h2.astype(h2_ref.dtype)
    _pack_rows(h2, h2p_ref)


def _out_proj(merged, w_out_b, x2, gate1, g2, shift2, scale2, seq):
    t, d = x2.shape
    tr = _token_rows(d)
    tm = 512
    per_b = seq // tm
    row = pl.BlockSpec((tm, d), lambda i: (i, 0))
    bvec = pl.BlockSpec((None, 1, d), lambda i: (i // per_b, 0, 0))
    return pl.pallas_call(
        _outproj_kernel,
        grid=(t // tm,),
        in_specs=[row, pl.BlockSpec((d, d), lambda i: (0, 0)), row, bvec,
                  pl.BlockSpec((1, d), lambda i: (0, 0)), bvec, bvec],
        out_specs=[row, row, pl.BlockSpec((tm * tr, LANES), lambda i: (i, 0))],
        out_shape=[jax.ShapeDtypeStruct((t, d), F32), jax.ShapeDtypeStruct((t, d), BF16),
                   jax.ShapeDtypeStruct((t * tr, LANES), U32)],
        compiler_params=_params("arbitrary"),
        name="out_proj",
    )(merged, w_out_b, x2, gate1, g2, shift2, scale2)


def _router_kernel(h_ref, rw_ref, bias_ref, idx_ref, w_ref, rank_ref, cnt_ref, carry):
    ne = rw_ref.shape[0]
    tm = h_ref.shape[0]
    gsz = ne // N_EXPERT_GROUPS

    @pl.when(pl.program_id(0) == 0)
    def _():
        carry[...] = jnp.zeros_like(carry)

    logits = lax.dot_general(rw_ref[...], h_ref[...].astype(BF16), (((1,), (1,)), ((), ())),
                             preferred_element_type=F32)
    scores = jax.nn.sigmoid(logits)
    sel = scores + bias_ref[...]
    row = lax.broadcasted_iota(I32, (ne, tm), 0)
    neg = -jnp.inf

    gscore = []
    rg = lax.broadcasted_iota(I32, (gsz, tm), 0)
    for g in range(N_EXPERT_GROUPS):
        sg = sel[g * gsz:(g + 1) * gsz, :]
        m1 = jnp.max(sg, axis=0, keepdims=True)
        i1 = jnp.min(jnp.where(sg == m1, rg, ne), axis=0, keepdims=True)
        m2 = jnp.max(jnp.where(rg == i1, neg, sg), axis=0, keepdims=True)
        gscore.append(m1 + m2)
    keep_rows = []
    for g in range(N_EXPERT_GROUPS):
        beaten = jnp.zeros((1, tm), I32)
        for o in range(N_EXPERT_GROUPS):
            if o == g:
                continue
            wins = (gscore[o] >= gscore[g]) if o < g else (gscore[o] > gscore[g])
            beaten = beaten + wins.astype(I32)
        keep_rows.append(jnp.broadcast_to(beaten, (gsz, tm)))
    cur = jnp.where(jnp.concatenate(keep_rows, axis=0) < TOPK_GROUPS, sel, neg)

    chosen = jnp.zeros((ne, tm), F32)
    picks, wts = [], []
    for _ in range(TOP_K):
        m = jnp.max(cur, axis=0, keepdims=True)
        ik = jnp.min(jnp.where(cur == m, row, ne), axis=0, keepdims=True)
        hit = row == ik
        wts.append(jnp.sum(jnp.where(hit, scores, 0.0), axis=0, keepdims=True))
        cur = jnp.where(hit, neg, cur)
        chosen = jnp.where(hit, 1.0, chosen)
        picks.append(ik)
    wsum = wts[0]
    for k in range(1, TOP_K):
        wsum = wsum + wts[k]

    ti = lax.broadcasted_iota(I32, (tm, tm), 0)
    tj = lax.broadcasted_iota(I32, (tm, tm), 1)
    upper = (ti < tj).astype(BF16)
    chosen_b = chosen.astype(BF16)
    before = jnp.dot(chosen_b, upper, preferred_element_type=F32)
    total = jnp.dot(chosen_b, jnp.ones((tm, LANES), BF16), preferred_element_type=F32)
    base = carry[...]
    pos = before + jnp.concatenate([base] * (tm // LANES), axis=1)
    for k in range(TOP_K):
        hit = row == picks[k]
        idx_ref[k:k + 1, :] = picks[k]
        w_ref[k:k + 1, :] = wts[k] / wsum * ROUTED_SCALE
        rank_ref[k:k + 1, :] = jnp.sum(jnp.where(hit, pos, 0.0), axis=0, keepdims=True).astype(I32)
    carry[...] = base + total
    cnt_ref[...] = base + total


def _router(h2, rw_t, bias_col):
    t, d = h2.shape
    ne = rw_t.shape[0]
    tm = 256
    kt = pl.BlockSpec((TOP_K, tm), lambda i: (0, i))
    return pl.pallas_call(
        _router_kernel,
        grid=(t // tm,),
        in_specs=[pl.BlockSpec((tm, d), lambda i: (i, 0)),
                  pl.BlockSpec((ne, d), lambda i: (0, 0)),
                  pl.BlockSpec((ne, 1), lambda i: (0, 0))],
        out_specs=[kt, kt, kt, pl.BlockSpec((ne, LANES), lambda i: (0, 0))],
        out_shape=[jax.ShapeDtypeStruct((TOP_K, t), I32), jax.ShapeDtypeStruct((TOP_K, t), F32),
                   jax.ShapeDtypeStruct((TOP_K, t), I32), jax.ShapeDtypeStruct((ne, LANES), F32)],
        scratch_shapes=[pltpu.VMEM((ne, LANES), F32)],
        compiler_params=_params("arbitrary"),
        name="router",
    )(h2, rw_t, bias_col)


def _dest_kernel(start_ref, idx_ref, rank_ref, dest_ref):
    ne = start_ref.shape[0]
    idx = idx_ref[...]

    def body(e, acc):
        return jnp.where(idx == e, start_ref[e], acc)

    dest = rank_ref[...] + lax.fori_loop(0, ne, body, jnp.zeros(idx.shape, I32))
    for j in range(dest_ref.shape[0]):
        dest_ref[j] = dest[:, j * DEST_TOKENS:(j + 1) * DEST_TOKENS]


def _dest_rows(starts, idx_t, rank_t):
    t = idx_t.shape[1]
    tb = 2048
    per = tb // DEST_TOKENS
    return pl.pallas_call(
        _dest_kernel,
        grid_spec=pltpu.PrefetchScalarGridSpec(
            num_scalar_prefetch=1,
            grid=(t // tb,),
            in_specs=[pl.BlockSpec((TOP_K, tb), lambda i, s: (0, i)),
                      pl.BlockSpec((TOP_K, tb), lambda i, s: (0, i))],
            out_specs=pl.BlockSpec((per, TOP_K, DEST_TOKENS), lambda i, s: (i, 0, 0)),
        ),
        out_shape=jax.ShapeDtypeStruct((t // DEST_TOKENS, TOP_K, DEST_TOKENS), I32),
        compiler_params=_params("arbitrary"),
        name="dest_rows",
    )(starts, idx_t, rank_t)


ROW_DMA_UNROLL = 4
DEST_PER_TILE = DEST_TOKENS * TOP_K


def _idx_copy(dest_hbm, idx_smem, isem, tile, s):
    return pltpu.make_async_copy(
        dest_hbm.at[tile], idx_smem.at[pl.ds(pl.multiple_of(s * DEST_PER_TILE, DEST_PER_TILE), DEST_PER_TILE)],
        isem.at[s])


def _token_rows_at(ref, token, tr):
    return ref.at[pl.ds(pl.multiple_of(token * tr, tr), tr), :]


def _dispatch_kernel(dest_hbm, hp_ref, h_ref, w1_ref, w3_ref, w2_ref, xs_hbm, ysh_ref, idx_smem, isem, dsem, *, tr):
    i = pl.program_id(0)
    n = pl.num_programs(0)
    tm = h_ref.shape[0]
    slot = i % 2
    idx_copy = functools.partial(_idx_copy, dest_hbm, idx_smem, isem)

    @pl.when(i == 0)
    def _():
        idx_copy(0, 0).start()

    idx_copy(i, slot).wait()

    @pl.when(i + 1 < n)
    def _():
        idx_copy(i + 1, 1 - slot).start()

    base = slot * DEST_PER_TILE

    def issue(quarter):
        for t in range(quarter * tm // 4, (quarter + 1) * tm // 4):
            for k in range(TOP_K):
                d = idx_smem[base + t * TOP_K + k]
                pltpu.make_async_copy(hp_ref.at[pl.ds(t * tr, tr), :], _token_rows_at(xs_hbm, d, tr),
                                      dsem).start(priority=k % 2)

    x = h_ref[...]
    half = w2_ref.shape[1] // 2
    issue(0)
    h1 = jnp.dot(x, w1_ref[...], preferred_element_type=F32)
    issue(1)
    h3 = jnp.dot(x, w3_ref[...], preferred_element_type=F32)
    act = (_silu(h1) * h3).astype(BF16)
    issue(2)
    ysh_ref[:, :half] = jnp.dot(act, w2_ref[:, :half], preferred_element_type=F32)
    issue(3)
    ysh_ref[:, half:] = jnp.dot(act, w2_ref[:, half:], preferred_element_type=F32)
    for k in range(TOP_K):
        pltpu.make_async_copy(hp_ref, xs_hbm.at[pl.ds(0, tm * tr), :], dsem).wait()


def _dispatch(dest, h2p, h2, w1_b, w3_b, w2_b):
    t, d = h2.shape
    de = w1_b.shape[1]
    tr = _token_rows(d)
    tm = DEST_TOKENS
    resident = lambda shape: pl.BlockSpec(shape, lambda i: (0, 0), pipeline_mode=pl.Buffered(1))
    return pl.pallas_call(
        functools.partial(_dispatch_kernel, tr=tr),
        grid=(t // tm,),
        in_specs=[pl.BlockSpec(memory_space=pl.ANY), pl.BlockSpec((tm * tr, LANES), lambda i: (i, 0)),
                  pl.BlockSpec((tm, d), lambda i: (i, 0)),
                  resident((d, de)), resident((d, de)), resident((de, d))],
        out_specs=[pl.BlockSpec(memory_space=pl.ANY), pl.BlockSpec((tm, d), lambda i: (i, 0))],
        out_shape=[jax.ShapeDtypeStruct((t * TOP_K * tr, LANES), U32), jax.ShapeDtypeStruct((t, d), F32)],
        scratch_shapes=[pltpu.SMEM((2 * DEST_PER_TILE,), I32), pltpu.SemaphoreType.DMA((2,)),
                        pltpu.SemaphoreType.DMA(())],
        compiler_params=_params("arbitrary"),
        name="dispatch_shared",
    )(dest, h2p, h2, w1_b, w3_b, w2_b)


def _experts_kernel(e_ref, b_ref, lo_ref, hi_ref, nxt_ref, half_ref, n_ref, xs_ref, w1_hbm, w3_hbm, w2_hbm, ys_ref,
                    w1s, w3s, w2s, w1b, w3b, w2b, xb, yp, wsem):
    w = pl.program_id(0)
    prev = jnp.maximum(w - 1, 0)

    def fetch(e):
        return (pltpu.make_async_copy(w1_hbm.at[e], w1s, wsem.at[0]),
                pltpu.make_async_copy(w3_hbm.at[e], w3s, wsem.at[1]),
                pltpu.make_async_copy(w2_hbm.at[e], w2s, wsem.at[2]))

    @pl.when(w < n_ref[0])
    def _():
        @pl.when(w == 0)
        def _():
            for cp in fetch(e_ref[0]):
                cp.start()

        @pl.when(jnp.logical_or(w == 0, e_ref[w] != e_ref[prev]))
        def _():
            for cp in fetch(e_ref[w]):
                cp.wait()
            w1b[...] = w1s[...].astype(BF16)
            w3b[...] = w3s[...].astype(BF16)
            w2b[...] = w2s[...].astype(BF16)

            @pl.when(nxt_ref[w] >= 0)
            def _():
                for cp in fetch(nxt_ref[w]):
                    cp.start()

        d = xb.shape[1]
        tr = _token_rows(d)
        new_block = jnp.logical_or(w == 0, b_ref[w] != b_ref[prev])

        @pl.when(new_block)
        def _():
            for s in range(tr):
                lo, hi = _unpack_slab(xs_ref, s, EXPERT_ROWS, tr)
                xb[:, s * LANES:(s + 1) * LANES] = lo.astype(BF16)
                xb[:, d // 2 + s * LANES:d // 2 + (s + 1) * LANES] = hi.astype(BF16)
            ys_ref[...] = jnp.zeros(ys_ref.shape, U32)

        first = (lo_ref[w] - b_ref[w] * EXPERT_ROWS) * tr
        last = (hi_ref[w] - b_ref[w] * EXPERT_ROWS) * tr

        def run(row0, nrows):
            x = xb[pl.ds(row0, nrows), :]
            h1 = jnp.dot(x, w1b[...], preferred_element_type=F32)
            h3 = jnp.dot(x, w3b[...], preferred_element_type=F32)
            act = (_silu(h1) * h3).astype(BF16)
            span = pl.ds(row0 * tr, nrows * tr)
            _pack_rows(jnp.dot(act, w2b[...], preferred_element_type=F32), yp.at[span, :])
            prow = row0 * tr + lax.broadcasted_iota(I32, (nrows * tr, 1), 0)
            mine = jnp.logical_and(prow >= first, prow < last)
            ys_ref[span, :] = jnp.where(mine, yp[span, :], ys_ref[span, :])

        half_rows = EXPERT_ROWS // 2

        @pl.when(half_ref[w] == 0)
        def _():
            run(0, EXPERT_ROWS)

        @pl.when(half_ref[w] != 0)
        def _():
            run(pl.multiple_of((half_ref[w] - 1) * half_rows, half_rows), half_rows)


def _experts(item_expert, item_block, item_lo, item_hi, item_next, item_half, n_items, xs, w1, w3, w2):
    _, d, de = w1.shape
    tr = _token_rows(d)
    rows = lambda w, e, b, lo, hi, nx, hf, n: (b[w], 0)
    hbm = pl.BlockSpec(memory_space=pl.ANY)
    return pl.pallas_call(
        _experts_kernel,
        grid_spec=pltpu.PrefetchScalarGridSpec(
            num_scalar_prefetch=7,
            grid=(item_expert.shape[0],),
            in_specs=[pl.BlockSpec((EXPERT_ROWS * tr, LANES), rows), hbm, hbm, hbm],
            out_specs=pl.BlockSpec((EXPERT_ROWS * tr, LANES), rows),
            scratch_shapes=[pltpu.VMEM((d, de), F32), pltpu.VMEM((d, de), F32), pltpu.VMEM((de, d), F32),
                            pltpu.VMEM((d, de), BF16), pltpu.VMEM((d, de), BF16), pltpu.VMEM((de, d), BF16),
                            pltpu.VMEM((EXPERT_ROWS, d), BF16), pltpu.VMEM((EXPERT_ROWS * tr, LANES), U32),
                            pltpu.SemaphoreType.DMA((3,))],
        ),
        out_shape=jax.ShapeDtypeStruct(xs.shape, U32),
        compiler_params=_params("arbitrary"),
        name="experts",
    )(item_expert, item_block, item_lo, item_hi, item_next, item_half, n_items, xs, w1, w3, w2)


def _combine_kernel(dest_hbm, ys_hbm, x1_ref, ysh_ref, wt_ref, gate_ref, fg_ref, o_ref,
                    rows0, rows1, idx_smem, isem, gsem):
    i = pl.program_id(0)
    n = pl.num_programs(0)
    tm, d = x1_ref.shape
    tr = _token_rows(d)
    slot = i % 2
    idx_copy = functools.partial(_idx_copy, dest_hbm, idx_smem, isem)

    def gather(buf, s, t, k):
        src = idx_smem[s * DEST_PER_TILE + t * TOP_K + k]
        return pltpu.make_async_copy(_token_rows_at(ys_hbm, src, tr), buf.at[k, pl.ds(t * tr, tr), :], gsem.at[s])

    def wait_gathers(buf, s):
        for k in range(TOP_K):
            pltpu.make_async_copy(ys_hbm.at[pl.ds(0, tm * tr), :], buf.at[k], gsem.at[s]).wait()

    @pl.when(i == 0)
    def _():
        idx_copy(0, 0).start()
        idx_copy(0, 0).wait()

        def body(c, carry):
            for u in range(ROW_DMA_UNROLL):
                for k in range(TOP_K):
                    gather(rows0, 0, c * ROW_DMA_UNROLL + u, k).start(priority=k % 2)
            return carry

        lax.fori_loop(0, tm // ROW_DMA_UNROLL, body, 0)

        @pl.when(n > 1)
        def _():
            idx_copy(1, 1).start()

    @pl.when(i + 1 < n)
    def _():
        idx_copy(i + 1, 1 - slot).wait()

    @pl.when(i + 2 < n)
    def _():
        idx_copy(i + 2, slot).start()

    def step(cur, cur_s, nxt, nxt_s):
        wait_gathers(cur, cur_s)
        w = wt_ref[...]
        ssq = jnp.zeros((tm, 1), F32)
        for s in range(tr):
            for t in range(s * tm // tr, (s + 1) * tm // tr):
                for k in range(TOP_K):
                    gather(nxt, nxt_s, t, k).start(priority=k % 2)
            cols = (slice(s * LANES, (s + 1) * LANES), slice(d // 2 + s * LANES, d // 2 + (s + 1) * LANES))
            acc = [ysh_ref[:, cs] for cs in cols]
            for k in range(TOP_K):
                halves = _unpack_slab(cur.at[k], s, tm, tr)
                acc = [a + v * w[:, k:k + 1] for a, v in zip(acc, halves)]
            for cs, a in zip(cols, acc):
                x2 = x1_ref[:, cs] + gate_ref[:, cs] * a
                o_ref[:, cs] = x2
                ssq = ssq + jnp.sum(x2 * x2, axis=-1, keepdims=True)
        o_ref[...] = o_ref[...] * lax.rsqrt(ssq * (1.0 / d) + NORM_EPS) * fg_ref[...]

        @pl.when(i + 1 == n)
        def _():
            wait_gathers(nxt, nxt_s)

    @pl.when(slot == 0)
    def _():
        step(rows0, 0, rows1, 1)

    @pl.when(slot == 1)
    def _():
        step(rows1, 1, rows0, 0)


def _combine(dest, ys, x1, ysh, w_tok, gate2, final_g, seq):
    t, d = x1.shape
    tr = _token_rows(d)
    tm = DEST_TOKENS
    assert t // tm >= 2, "the gather ring keeps two token tiles in flight"
    per_b = seq // tm
    row = pl.BlockSpec((tm, d), lambda i: (i, 0))
    return pl.pallas_call(
        _combine_kernel,
        grid=(t // tm,),
        in_specs=[pl.BlockSpec(memory_space=pl.ANY), pl.BlockSpec(memory_space=pl.ANY), row, row,
                  pl.BlockSpec((tm, TOP_K), lambda i: (i, 0)),
                  pl.BlockSpec((None, 1, d), lambda i: (i // per_b, 0, 0)),
                  pl.BlockSpec((1, d), lambda i: (0, 0))],
        out_specs=row,
        out_shape=jax.ShapeDtypeStruct((t, d), F32),
        scratch_shapes=[pltpu.VMEM((TOP_K, tm * tr, LANES), U32), pltpu.VMEM((TOP_K, tm * tr, LANES), U32),
                        pltpu.SMEM((2 * DEST_PER_TILE,), I32),
                        pltpu.SemaphoreType.DMA((2,)), pltpu.SemaphoreType.DMA((2,))],
        compiler_params=_params("arbitrary"),
        name="combine",
    )(dest, ys, x1, ysh, w_tok, gate2, final_g)


def _mixer(x2, mod6, cos_t, sin_t, bsz, seq, p):
    t, d = x2.shape
    shift1, scale1, gate1, shift2, scale2, _ = mod6
    d_rnn = p["conv_w"].shape[1]
    att_width = len(DILATION_GROUPS) * GROUP_COLS
    q_col = 2 * d_rnn
    gate_col = 2 * d_rnn + 3 * att_width

    def qkv_weights(g):
        parts = [p["w_in"][:, q_col + j * att_width + g * GROUP_COLS:q_col + j * att_width + (g + 1) * GROUP_COLS]
                 for j in range(3)]
        return jnp.concatenate(parts, axis=1).astype(BF16)

    h1 = _norm_mod(x2, p["norm1_g"].reshape(1, d), shift1, scale1, seq)
    xr = _proj_act(h1, p["w_in"][:, :d_rnn].astype(BF16), None, "proj_rnn_x")
    gr = _proj_act(h1, p["w_in"][:, d_rnn:2 * d_rnn].astype(BF16), _gelu_tanh, "proj_rnn_gate")
    gates = _proj_act(h1, p["w_in"][:, gate_col:].astype(BF16), _sigmoid_tanh, "proj_gates")
    ya = _rglru(xr, gr, p["conv_w"], p["conv_b"].reshape(1, d_rnn),
                (0.5 * p["rg_wa"]).astype(BF16), 0.5 * p["rg_ba"].reshape(1, d_rnn),
                (0.5 * p["rg_wi"]).astype(BF16), 0.5 * p["rg_bi"].reshape(1, d_rnn),
                p["rg_lambda"].reshape(1, d_rnn), bsz, seq, d_rnn)

    outs, lses = [], []
    for g, (window, dilation) in enumerate(DILATION_GROUPS):
        qkv = _proj_qkv(h1, qkv_weights(g), cos_t, sin_t, g, dilation, bsz, seq)
        o, l = _attention_group(qkv, g, window, dilation)
        outs.append(o)
        lses.append(l)

    merged = _merge(ya, outs, lses, gates, p["w_proj_rnn"].astype(BF16), p["w_proj_attn"].astype(BF16), seq)
    return _out_proj(merged, p["w_out"].astype(BF16), x2, gate1, p["norm2_g"].reshape(1, d),
                     shift2, scale2, seq)


def _moe(h2, h2p, p):
    t, d = h2.shape
    ne = p["router_w"].shape[1]
    idx_t, w_t, rank_t, cnt = _router(h2, p["router_w"].T.astype(BF16), p["router_bias"].reshape(ne, 1))

    counts = cnt[:, 0].astype(I32)
    ends = jnp.cumsum(counts).astype(I32)
    starts = ends - counts
    n_rows = t * TOP_K
    first_blk = starts // EXPERT_ROWS
    n_blk_e = jnp.where(counts > 0, (ends - 1) // EXPERT_ROWS - first_blk + 1, 0)
    item_end = jnp.cumsum(n_blk_e).astype(I32)
    item_start = item_end - n_blk_e
    n_items = item_end[-1]
    max_items = n_rows // EXPERT_ROWS + ne
    w = jnp.minimum(jnp.arange(max_items, dtype=I32), n_items - 1)
    owner = lambda i: jnp.minimum(jnp.sum((item_end[None, :] <= i[:, None]).astype(I32), axis=1), ne - 1)
    item_expert = owner(w)
    onehot = item_expert[:, None] == jnp.arange(ne, dtype=I32)[None, :]
    pick = lambda table: jnp.sum(jnp.where(onehot, table[None, :], 0), axis=1).astype(I32)
    item_block = pick(first_blk) + (w - pick(item_start))
    after = pick(item_end)
    item_next = jnp.where(after < n_items, owner(after), -1).astype(I32)

    dest = _dest_rows(starts, idx_t, rank_t)
    dest = jnp.transpose(dest, (0, 2, 1)).reshape(t // DEST_TOKENS, DEST_PER_TILE)
    xs, ysh = _dispatch(dest, h2p, h2, p["sh_w1"].astype(BF16), p["sh_w3"].astype(BF16), p["sh_w2"].astype(BF16))
    item_lo, item_hi = pick(starts), pick(ends)
    in_lo = jnp.maximum(item_lo, item_block * EXPERT_ROWS) - item_block * EXPERT_ROWS
    in_hi = jnp.minimum(item_hi, (item_block + 1) * EXPERT_ROWS) - item_block * EXPERT_ROWS
    item_half = jnp.where(in_hi <= EXPERT_ROWS // 2, 1, jnp.where(in_lo >= EXPERT_ROWS // 2, 2, 0)).astype(I32)
    ys = _experts(item_expert, item_block, item_lo, item_hi, item_next, item_half,
                  n_items.reshape(1), xs, p["exp_w1"], p["exp_w3"], p["exp_w2"])
    return dest, ys, ysh, w_t.T


def kernel(x, c, positions, ada_w, ada_b, norm1_g, w_in, conv_w, conv_b, rg_wa, rg_ba, rg_wi, rg_bi, rg_lambda, w_proj_rnn, w_proj_attn, w_out, norm2_g, router_w, router_bias, exp_w1, exp_w3, exp_w2, sh_w1, sh_w3, sh_w2, final_g):
    bsz, seq, d = x.shape
    assert ada_w.shape[0] == 1, "the fused final norm assumes a single layer"
    t = bsz * seq
    x2 = x.reshape(t, d)
    first = lambda a: a.reshape(a.shape[1:])

    half = HEAD_DIM // 2
    inv_freq = ROPE_THETA ** (-jnp.arange(half, dtype=F32) * 2.0 / HEAD_DIM)
    freq = jnp.concatenate([inv_freq, inv_freq]).reshape(1, HEAD_DIM)
    sign = jnp.concatenate([-jnp.ones((half,), F32), jnp.ones((half,), F32)]).reshape(1, HEAD_DIM)
    cos_t, sin_t = _rope_tables(positions.reshape(t, 1), freq, sign)

    c_pad = jnp.zeros((SUBLANES, d), F32).at[:bsz].set(c)
    mod = _ada_mod(c_pad, first(ada_w), ada_b.reshape(1, -1))
    mod6 = tuple(mod[:bsz, k * d:(k + 1) * d].reshape(bsz, 1, d) for k in range(6))

    p = dict(norm1_g=first(norm1_g), w_in=first(w_in), conv_w=first(conv_w), conv_b=first(conv_b),
             rg_wa=first(rg_wa), rg_ba=first(rg_ba), rg_wi=first(rg_wi), rg_bi=first(rg_bi),
             rg_lambda=first(rg_lambda), w_proj_rnn=first(w_proj_rnn), w_proj_attn=first(w_proj_attn),
             w_out=first(w_out), norm2_g=first(norm2_g), router_w=first(router_w),
             router_bias=first(router_bias), exp_w1=first(exp_w1), exp_w3=first(exp_w3),
             exp_w2=first(exp_w2), sh_w1=first(sh_w1), sh_w3=first(sh_w3), sh_w2=first(sh_w2))
    x1, h2, h2p = _mixer(x2, mod6, cos_t, sin_t, bsz, seq, p)
    dest, ys, ysh, w_tok = _moe(h2, h2p, p)
    out = _combine(dest, ys, x1, ysh, w_tok, mod6[5], final_g.reshape(1, d), seq)
    return out.reshape(bsz, seq, d)
```

```python
import functools
import math

import jax
import jax.numpy as jnp
import numpy as np
from jax import lax
from jax.experimental import pallas as pl
from jax.experimental.pallas import tpu as pltpu

F32 = jnp.float32
BF16 = jnp.bfloat16
I32 = jnp.int32

HEAD_DIM = 128
HEADS_PER_GROUP = 4
DILATION_GROUPS = ((128, 1), (512, 4), (2048, 16))
ROPE_THETA = 10000.0
CONV_WIDTH = 4
LRU_C = 8.0
TOP_K = 8
N_EXPERT_GROUPS = 8
TOPK_GROUPS = 4
ROUTED_SCALE = 2.5
NORM_EPS = 1e-6

LANES = 128
SUBLANES = 8
VMEM_LIMIT_BYTES = 56 * 1024 * 1024

GROUP_COLS = HEADS_PER_GROUP * HEAD_DIM
EXPERT_ROWS = 256
DEST_TOKENS = 128


def _params(*sem):
    return pltpu.CompilerParams(dimension_semantics=sem, vmem_limit_bytes=VMEM_LIMIT_BYTES)


def _gelu_tanh(x):
    return 0.5 * x * (1.0 + jnp.tanh(math.sqrt(2.0 / math.pi) * (x + 0.044715 * (x * x * x))))


def _silu(x):
    return x * jax.nn.sigmoid(x)


def _rms(x, g):
    ms = jnp.mean(x * x, axis=-1, keepdims=True)
    return x * lax.rsqrt(ms + NORM_EPS) * g


def _ada_kernel(c_ref, w_ref, b_ref, o_ref):
    a = _silu(c_ref[...]).astype(BF16)
    o_ref[...] = jnp.dot(a, w_ref[...].astype(BF16), preferred_element_type=F32) + b_ref[...]


def _ada_mod(c_pad, ada_w, ada_b):
    rows, d = c_pad.shape
    n = ada_w.shape[1]
    tn = 1024
    return pl.pallas_call(
        _ada_kernel,
        grid=(n // tn,),
        in_specs=[
            pl.BlockSpec((rows, d), lambda j: (0, 0)),
            pl.BlockSpec((d, tn), lambda j: (0, j)),
            pl.BlockSpec((1, tn), lambda j: (0, j)),
        ],
        out_specs=pl.BlockSpec((rows, tn), lambda j: (0, j)),
        out_shape=jax.ShapeDtypeStruct((rows, n), F32),
        compiler_params=_params("arbitrary"),
        name="ada_mod",
    )(c_pad, ada_w, ada_b)


def _rope_kernel(pos_ref, freq_ref, sign_ref, cos_ref, sin_ref):
    ang = pos_ref[...].astype(F32) * freq_ref[...]
    cos_ref[...] = jnp.cos(ang)
    sin_ref[...] = jnp.sin(ang) * sign_ref[...]


def _rope_tables(pos_col, freq, sign):
    t = pos_col.shape[0]
    tm = 1024
    return pl.pallas_call(
        _rope_kernel,
        grid=(t // tm,),
        in_specs=[
            pl.BlockSpec((tm, 1), lambda i: (i, 0)),
            pl.BlockSpec((1, HEAD_DIM), lambda i: (0, 0)),
            pl.BlockSpec((1, HEAD_DIM), lambda i: (0, 0)),
        ],
        out_specs=[pl.BlockSpec((tm, HEAD_DIM), lambda i: (i, 0))] * 2,
        out_shape=[jax.ShapeDtypeStruct((t, HEAD_DIM), F32)] * 2,
        compiler_params=_params("arbitrary"),
        name="rope_tables",
    )(pos_col, freq, sign)


def _norm_kernel(x_ref, g_ref, sh_ref, sc_ref, h_ref):
    h_ref[...] = (_rms(x_ref[...], g_ref[...]) * (1.0 + sc_ref[...]) + sh_ref[...]).astype(h_ref.dtype)


def _norm_mod(x2, g, shift, scale, seq):
    t, d = x2.shape
    tm = 1024
    per_b = seq // tm
    bvec = pl.BlockSpec((None, 1, d), lambda i: (i // per_b, 0, 0))
    return pl.pallas_call(
        _norm_kernel,
        grid=(t // tm,),
        in_specs=[pl.BlockSpec((tm, d), lambda i: (i, 0)), pl.BlockSpec((1, d), lambda i: (0, 0)), bvec, bvec],
        out_specs=pl.BlockSpec((tm, d), lambda i: (i, 0)),
        out_shape=jax.ShapeDtypeStruct((t, d), BF16),
        compiler_params=_params("arbitrary"),
        name="norm_mod",
    )(x2, g, shift, scale)


def _proj_act_kernel(h_ref, w_ref, o_ref, *, act):
    acc = jnp.dot(h_ref[...], w_ref[...], preferred_element_type=F32)
    o_ref[...] = (acc if act is None else act(acc)).astype(o_ref.dtype)


def _sigmoid_tanh(x):
    return 0.5 + 0.5 * jnp.tanh(0.5 * x)


def _proj_act(h1, w_b, act, name):
    t, d = h1.shape
    ncols = w_b.shape[1]
    tn = 1024 if ncols % 1024 == 0 else GROUP_COLS
    tm = 2048 if (tn == 1024 and t % 2048 == 0) else 1024
    return pl.pallas_call(
        functools.partial(_proj_act_kernel, act=act),
        grid=(t // tm, ncols // tn),
        in_specs=[pl.BlockSpec((tm, d), lambda i, j: (i, 0)),
                  pl.BlockSpec((d, tn), lambda i, j: (0, j))],
        out_specs=pl.BlockSpec((tm, tn), lambda i, j: (i, j)),
        out_shape=jax.ShapeDtypeStruct((t, ncols), BF16),
        compiler_params=_params("arbitrary", "arbitrary"),
        name=name,
    )(h1, w_b)


def _proj_qkv_kernel(h_ref, w_ref, cos_ref, sin_ref, o_ref, scr, *, dilation):
    tm = h_ref.shape[0]
    acc = jnp.dot(h_ref[...], w_ref[...], preferred_element_type=F32)
    j = pl.program_id(1)
    is_v = j == 2
    scale = jnp.where(j == 0, HEAD_DIM ** -0.5, 1.0).astype(F32)
    c = jnp.where(is_v, 1.0, cos_ref[...] * scale)
    s = jnp.where(is_v, 0.0, sin_ref[...] * scale)
    sub = tm // dilation
    for h in range(HEADS_PER_GROUP):
        hs = slice(h * HEAD_DIM, (h + 1) * HEAD_DIM)
        v = acc[:, hs]
        res = v * c + pltpu.roll(v, HEAD_DIM // 2, axis=1) * s
        if dilation == 1:
            o_ref[0, :, hs] = res.astype(o_ref.dtype)
        else:
            scr[h] = res
    if dilation > 1:
        for r in range(dilation):
            for h in range(HEADS_PER_GROUP):
                rows = scr[h, pl.ds(r, sub, stride=dilation), :]
                o_ref[r, :, h * HEAD_DIM:(h + 1) * HEAD_DIM] = rows.astype(o_ref.dtype)


def _proj_qkv(h1, w_qkv_b, cos_t, sin_t, g, dilation, bsz, seq):
    t, d = h1.shape
    tm, tn = (2048 if seq % 2048 == 0 else 1024), GROUP_COLS
    per_b = seq // tm
    sub = tm // dilation
    return pl.pallas_call(
        functools.partial(_proj_qkv_kernel, dilation=dilation),
        grid=(t // tm, 3),
        in_specs=[pl.BlockSpec((tm, d), lambda i, j: (i, 0)),
                  pl.BlockSpec((d, tn), lambda i, j: (0, j)),
                  pl.BlockSpec((tm, HEAD_DIM), lambda i, j: (i, 0)),
                  pl.BlockSpec((tm, HEAD_DIM), lambda i, j: (i, 0))],
        out_specs=pl.BlockSpec((None, dilation, sub, tn), lambda i, j: (i // per_b, 0, i % per_b, j)),
        out_shape=jax.ShapeDtypeStruct((bsz, dilation, seq // dilation, 3 * tn), BF16),
        scratch_shapes=[pltpu.VMEM((HEADS_PER_GROUP, tm, HEAD_DIM), F32)],
        compiler_params=_params("arbitrary", "arbitrary"),
        name=f"proj_qkv_g{g}",
    )(h1, w_qkv_b, cos_t, sin_t)


def _rglru_kernel(xr_ref, gr_ref, cw_ref, cb_ref, wa_ref, ba_ref, wi_ref, bi_ref, lam_ref,
                  ya_ref, xbuf, a_scr, b_scr, hcar):
    tt = xr_ref.shape[0]
    nc = xbuf.shape[0]
    halo = SUBLANES
    seg = tt // SUBLANES
    pitch = a_scr.shape[1] // SUBLANES
    lanes = lambda c: slice(c * LANES, (c + 1) * LANES)

    @pl.when(pl.program_id(2) == 0)
    def _():
        for c in range(nc):
            xbuf[c, 0:halo, :] = jnp.zeros((halo, LANES), F32)
        hcar[...] = jnp.zeros_like(hcar)

    for c in range(nc):
        cs = lanes(c)
        xbuf[c, halo:halo + tt, :] = xr_ref[:, cs].astype(F32)
        z = -lam_ref[:, cs]
        softplus = jnp.maximum(z, 0.0) + jnp.log1p(jnp.exp(-jnp.abs(z)))
        quarter = (-0.25 * LRU_C) * softplus
        for s in range(SUBLANES):
            r0 = halo + s * seg
            u = cb_ref[:, cs] + cw_ref[CONV_WIDTH - 1:CONV_WIDTH, cs] * xbuf[c, r0:r0 + seg, :]
            for j in range(CONV_WIDTH - 1):
                back = CONV_WIDTH - 1 - j
                u = u + cw_ref[j:j + 1, cs] * xbuf[c, r0 - back:r0 - back + seg, :]
            ub = u.astype(BF16)
            tr = jnp.tanh(jnp.dot(ub, wa_ref[c], preferred_element_type=F32) + ba_ref[:, cs])
            ti = jnp.tanh(jnp.dot(ub, wi_ref[c], preferred_element_type=F32) + bi_ref[:, cs])
            t = jnp.tanh(quarter + quarter * tr)
            q = 1.0 / (1.0 - t)
            a_scr[c, s * pitch:s * pitch + seg, :] = (1.0 + t) * q
            b_scr[c, s * pitch:s * pitch + seg, :] = q * jnp.sqrt(-t) * (u + u * ti)
        xbuf[c, 0:halo, :] = xbuf[c, tt:tt + halo, :]

    h = [jnp.zeros((SUBLANES, LANES), F32)] * nc
    prod = [jnp.ones((SUBLANES, LANES), F32)] * nc
    for j in range(seg):
        rows_j = pl.ds(j, SUBLANES, stride=pitch)
        for c in range(nc):
            aj = a_scr[c, rows_j, :]
            h[c] = aj * h[c] + b_scr[c, rows_j, :]
            prod[c] = aj * prod[c]
            b_scr[c, rows_j, :] = h[c]
            a_scr[c, rows_j, :] = prod[c]
    for c in range(nc):
        carry = hcar[0:1, lanes(c)]
        for s in range(SUBLANES):
            rs = slice(s * seg, (s + 1) * seg)
            ps = slice(s * pitch, s * pitch + seg)
            state = b_scr[c, ps, :] + a_scr[c, ps, :] * carry
            ya_ref[rs, lanes(c)] = (state * gr_ref[rs, lanes(c)].astype(F32)).astype(ya_ref.dtype)
            carry = h[c][s:s + 1, :] + prod[c][s:s + 1, :] * carry
        hcar[:, lanes(c)] = jnp.broadcast_to(carry, (SUBLANES, LANES))


RGLRU_CHANNEL_TILES = 4


def _rglru(xr, gr, conv_w, conv_b, wa_b, ba, wi_b, bi, lam, bsz, seq, d_rnn):
    t = xr.shape[0]
    tt = 512
    nc = RGLRU_CHANNEL_TILES
    width = nc * LANES
    per_b = seq // tt
    row = pl.BlockSpec((tt, width), lambda b, c, s: (b * per_b + s, c))
    vec = pl.BlockSpec((1, width), lambda b, c, s: (0, c))
    gate_w = pl.BlockSpec((nc, LANES, LANES), lambda b, c, s: (c, 0, 0))
    scan = pltpu.VMEM((nc, tt + SUBLANES * SUBLANES, LANES), F32)
    return pl.pallas_call(
        _rglru_kernel,
        grid=(bsz, d_rnn // width, per_b),
        in_specs=[row, row, pl.BlockSpec((CONV_WIDTH, width), lambda b, c, s: (0, c)), vec,
                  gate_w, vec, gate_w, vec, vec],
        out_specs=row,
        out_shape=jax.ShapeDtypeStruct((t, d_rnn), BF16),
        scratch_shapes=[pltpu.VMEM((nc, tt + SUBLANES, LANES), F32), scan, scan, pltpu.VMEM((SUBLANES, width), F32)],
        compiler_params=_params("arbitrary", "arbitrary", "arbitrary"),
        name="rglru",
    )(xr, gr, conv_w, conv_b, wa_b, ba, wi_b, bi, lam)


ATTN_BLOCKS = 4


def _attn_kernel(q_ref, kc_ref, kp_ref, vc_ref, vp_ref, o_ref, l_ref, *, blk):
    nq = q_ref.shape[0] // blk
    not_first = pl.program_id(2) > 0
    qi = lax.broadcasted_iota(I32, (blk, blk), 0)
    kj = lax.broadcasted_iota(I32, (blk, blk), 1)
    tri_prev = kj >= qi
    mask_cur = kj <= qi
    nt = (((1,), (1,)), ((), ()))
    units = [(j, h) for j in range(nq) for h in range(HEADS_PER_GROUP)]
    rows = lambda j: slice(j * blk, (j + 1) * blk)
    cols = lambda h: slice(h * HEAD_DIM, (h + 1) * HEAD_DIM)

    def prev_kv(ref, pref, j, h):
        return pref[:, cols(h)] if j == 0 else ref[rows(j - 1), cols(h)]

    scores = []
    for j, h in units:
        q = q_ref[rows(j), cols(h)]
        sp = lax.dot_general(q, prev_kv(kc_ref, kp_ref, j, h), nt, preferred_element_type=F32)
        sc = lax.dot_general(q, kc_ref[rows(j), cols(h)], nt, preferred_element_type=F32)
        mask_prev = jnp.logical_and(tri_prev, not_first) if j == 0 else tri_prev
        scores.append((jnp.where(mask_prev, sp, -jnp.inf), jnp.where(mask_cur, sc, -jnp.inf)))
    maxes = [jnp.maximum(jnp.max(sp, axis=-1, keepdims=True), jnp.max(sc, axis=-1, keepdims=True))
             for sp, sc in scores]
    probs = [(jnp.exp(sp - m), jnp.exp(sc - m)) for (sp, sc), m in zip(scores, maxes)]
    dens = [jnp.sum(pp, axis=-1, keepdims=True) + jnp.sum(pc, axis=-1, keepdims=True) for pp, pc in probs]
    for (j, h), (pp, pc), m, den in zip(units, probs, maxes, dens):
        inv = 1.0 / den
        out = (jnp.dot((pp * inv).astype(BF16), prev_kv(vc_ref, vp_ref, j, h), preferred_element_type=F32)
               + jnp.dot((pc * inv).astype(BF16), vc_ref[rows(j), cols(h)], preferred_element_type=F32))
        o_ref[rows(j), cols(h)] = out.astype(o_ref.dtype)
        l_ref[rows(j), cols(h)] = jnp.broadcast_to(m + jnp.log(den), (blk, HEAD_DIM))


def _attention_group(qkv, g, window, dilation):
    bsz, _, length, _ = qkv.shape
    blk = window // dilation
    nb = length // blk
    nq = math.gcd(ATTN_BLOCKS, nb)
    cur = lambda c: (lambda b, r, n: (b, r, n, c))
    prev = lambda c: (lambda b, r, n: (b, r, jnp.maximum(n * nq - 1, 0), c))
    spec = lambda f: pl.BlockSpec((None, None, nq * blk, GROUP_COLS), f)
    pspec = lambda f: pl.BlockSpec((None, None, blk, GROUP_COLS), f)
    out_sds = lambda dt: jax.ShapeDtypeStruct((bsz, dilation, length, GROUP_COLS), dt)
    return pl.pallas_call(
        functools.partial(_attn_kernel, blk=blk),
        grid=(bsz, dilation, nb // nq),
        in_specs=[spec(cur(0)), spec(cur(1)), pspec(prev(1)), spec(cur(2)), pspec(prev(2))],
        out_specs=[spec(cur(0)), spec(cur(0))],
        out_shape=[out_sds(BF16), out_sds(F32)],
        compiler_params=_params("arbitrary", "arbitrary", "arbitrary"),
        name=f"attn_g{g}",
    )(qkv, qkv, qkv, qkv, qkv)


def _merge_kernel(ya_ref, o0, o1, o2, l0, l1, l2, g_ref, wr_ref, wa_ref, m_ref, yb_scr, o_scr, l_scr):
    tm = ya_ref.shape[0]
    d = m_ref.shape[1]
    for g, (o_ref, l_ref) in enumerate(((o0, l0), (o1, l1), (o2, l2))):
        dil = o_ref.shape[0]
        for r in range(dil):
            for h in range(HEADS_PER_GROUP):
                cs = slice(h * HEAD_DIM, (h + 1) * HEAD_DIM)
                if dil == 1:
                    o_scr[g, h] = o_ref[r, :, cs].astype(F32)
                    l_scr[g, h] = l_ref[r, :, cs]
                else:
                    o_scr[g, h, pl.ds(r, tm // dil, stride=dil), :] = o_ref[r, :, cs].astype(F32)
                    l_scr[g, h, pl.ds(r, tm // dil, stride=dil), :] = l_ref[r, :, cs]
    for h in range(HEADS_PER_GROUP):
        la, lb, lc = l_scr[0, h], l_scr[1, h], l_scr[2, h]
        m = jnp.maximum(jnp.maximum(la, lb), lc)
        ea, eb, ec = jnp.exp(la - m), jnp.exp(lb - m), jnp.exp(lc - m)
        inv = 1.0 / (ea + eb + ec)
        yb = (ea * inv) * o_scr[0, h] + (eb * inv) * o_scr[1, h] + (ec * inv) * o_scr[2, h]
        yb_scr[:, h * HEAD_DIM:(h + 1) * HEAD_DIM] = yb.astype(BF16)

    ya = ya_ref[...]
    yb = yb_scr[...]
    for c in range(d // GROUP_COLS):
        cs = slice(c * GROUP_COLS, (c + 1) * GROUP_COLS)
        gs = slice(d + c * GROUP_COLS, d + (c + 1) * GROUP_COLS)
        pa = jnp.dot(ya, wr_ref[:, cs], preferred_element_type=F32)
        pb = jnp.dot(yb, wa_ref[:, cs], preferred_element_type=F32)
        m_ref[:, cs] = (g_ref[:, cs].astype(F32) * pa + g_ref[:, gs].astype(F32) * pb).astype(m_ref.dtype)


def _merge(ya, outs, lses, gates, wr_b, wa_b, seq):
    t, d_rnn = ya.shape
    d = wr_b.shape[1]
    tm = 512
    per_b = seq // tm
    n_groups = len(outs)
    resident = lambda shape: pl.BlockSpec(shape, lambda i: (0, 0), pipeline_mode=pl.Buffered(1))

    def grp(o):
        dil = o.shape[1]
        return pl.BlockSpec((None, dil, tm // dil, GROUP_COLS), lambda i: (i // per_b, 0, i % per_b, 0))

    return pl.pallas_call(
        _merge_kernel,
        grid=(t // tm,),
        in_specs=[
            pl.BlockSpec((tm, d_rnn), lambda i: (i, 0)),
            *[grp(o) for o in outs], *[grp(l) for l in lses],
            pl.BlockSpec((tm, 2 * d), lambda i: (i, 0)),
            resident((d_rnn, d)), resident((GROUP_COLS, d)),
        ],
        out_specs=pl.BlockSpec((tm, d), lambda i: (i, 0)),
        out_shape=jax.ShapeDtypeStruct((t, d), BF16),
        scratch_shapes=[pltpu.VMEM((tm, GROUP_COLS), BF16),
                        pltpu.VMEM((n_groups, HEADS_PER_GROUP, tm, HEAD_DIM), F32),
                        pltpu.VMEM((n_groups, HEADS_PER_GROUP, tm, HEAD_DIM), F32)],
        compiler_params=_params("arbitrary"),
        name="merge_proj",
    )(ya, *outs, *lses, gates, wr_b, wa_b)


U32 = jnp.uint32
HIGH_HALF = np.uint32(0xFFFF0000)


def _token_rows(d):
    assert d % (2 * LANES) == 0
    return d // (2 * LANES)


def _pack_rows(v, dst_ref):
    rows, d = v.shape
    tr = _token_rows(d)
    lo = pltpu.bitcast(v[:, :d // 2].astype(BF16).astype(F32), U32)
    hi = pltpu.bitcast(v[:, d // 2:].astype(BF16).astype(F32), U32)
    word = (lo >> 16) | (hi & HIGH_HALF)
    for s in range(tr):
        dst_ref[pl.ds(s, rows, stride=tr), :] = word[:, s * LANES:(s + 1) * LANES]


def _unpack_slab(src_ref, s, rows, tr):
    word = src_ref[pl.ds(s, rows, stride=tr), :]
    return pltpu.bitcast(word << 16, F32), pltpu.bitcast(word & HIGH_HALF, F32)


def _outproj_kernel(m_ref, w_ref, x_ref, gate_ref, g2_ref, sh_ref, sc_ref, x1_ref, h2_ref, h2p_ref):
    x1 = x_ref[...] + gate_ref[...] * jnp.dot(m_ref[...], w_ref[...], preferred_element_type=F32)
    x1_ref[...] = x1
    h2 = _rms(x1, g2_ref[...]) * (1.0 + sc_ref[...]) + sh_ref[...]
    h2_ref[...] = h2.astype(h2_ref.dtype)
    _pack_rows(h2, h2p_ref)


def _out_proj(merged, w_out_b, x2, gate1, g2, shift2, scale2, seq):
    t, d = x2.shape
    tr = _token_rows(d)
    tm = 512
    per_b = seq // tm
    row = pl.BlockSpec((tm, d), lambda i: (i, 0))
    bvec = pl.BlockSpec((None, 1, d), lambda i: (i // per_b, 0, 0))
    return pl.pallas_call(
        _outproj_kernel,
        grid=(t // tm,),
        in_specs=[row, pl.BlockSpec((d, d), lambda i: (0, 0)), row, bvec,
                  pl.BlockSpec((1, d), lambda i: (0, 0)), bvec, bvec],
        out_specs=[row, row, pl.BlockSpec((tm * tr, LANES), lambda i: (i, 0))],
        out_shape=[jax.ShapeDtypeStruct((t, d), F32), jax.ShapeDtypeStruct((t, d), BF16),
                   jax.ShapeDtypeStruct((t * tr, LANES), U32)],
        compiler_params=_params("arbitrary"),
        name="out_proj",
    )(merged, w_out_b, x2, gate1, g2, shift2, scale2)


def _router_kernel(h_ref, rw_ref, bias_ref, idx_ref, w_ref, rank_ref, cnt_ref, carry):
    ne = rw_ref.shape[0]
    tm = h_ref.shape[0]
    gsz = ne // N_EXPERT_GROUPS

    @pl.when(pl.program_id(0) == 0)
    def _():
        carry[...] = jnp.zeros_like(carry)

    logits = lax.dot_general(rw_ref[...], h_ref[...].astype(BF16), (((1,), (1,)), ((), ())),
                             preferred_element_type=F32)
    scores = jax.nn.sigmoid(logits)
    sel = scores + bias_ref[...]
    row = lax.broadcasted_iota(I32, (ne, tm), 0)
    neg = -jnp.inf

    gscore = []
    rg = lax.broadcasted_iota(I32, (gsz, tm), 0)
    for g in range(N_EXPERT_GROUPS):
        sg = sel[g * gsz:(g + 1) * gsz, :]
        m1 = jnp.max(sg, axis=0, keepdims=True)
        i1 = jnp.min(jnp.where(sg == m1, rg, ne), axis=0, keepdims=True)
        m2 = jnp.max(jnp.where(rg == i1, neg, sg), axis=0, keepdims=True)
        gscore.append(m1 + m2)
    keep_rows = []
    for g in range(N_EXPERT_GROUPS):
        beaten = jnp.zeros((1, tm), I32)
        for o in range(N_EXPERT_GROUPS):
            if o == g:
                continue
            wins = (gscore[o] >= gscore[g]) if o < g else (gscore[o] > gscore[g])
            beaten = beaten + wins.astype(I32)
        keep_rows.append(jnp.broadcast_to(beaten, (gsz, tm)))
    cur = jnp.where(jnp.concatenate(keep_rows, axis=0) < TOPK_GROUPS, sel, neg)

    chosen = jnp.zeros((ne, tm), F32)
    picks, wts = [], []
    for _ in range(TOP_K):
        m = jnp.max(cur, axis=0, keepdims=True)
        ik = jnp.min(jnp.where(cur == m, row, ne), axis=0, keepdims=True)
        hit = row == ik
        wts.append(jnp.sum(jnp.where(hit, scores, 0.0), axis=0, keepdims=True))
        cur = jnp.where(hit, neg, cur)
        chosen = jnp.where(hit, 1.0, chosen)
        picks.append(ik)
    wsum = wts[0]
    for k in range(1, TOP_K):
        wsum = wsum + wts[k]

    ti = lax.broadcasted_iota(I32, (tm, tm), 0)
    tj = lax.broadcasted_iota(I32, (tm, tm), 1)
    upper = (ti < tj).astype(BF16)
    chosen_b = chosen.astype(BF16)
    before = jnp.dot(chosen_b, upper, preferred_element_type=F32)
    total = jnp.dot(chosen_b, jnp.ones((tm, LANES), BF16), preferred_element_type=F32)
    base = carry[...]
    pos = before + jnp.concatenate([base] * (tm // LANES), axis=1)
    for k in range(TOP_K):
        hit = row == picks[k]
        idx_ref[k:k + 1, :] = picks[k]
        w_ref[k:k + 1, :] = wts[k] / wsum * ROUTED_SCALE
        rank_ref[k:k + 1, :] = jnp.sum(jnp.where(hit, pos, 0.0), axis=0, keepdims=True).astype(I32)
    carry[...] = base + total
    cnt_ref[...] = base + total


def _router(h2, rw_t, bias_col):
    t, d = h2.shape
    ne = rw_t.shape[0]
    tm = 256
    kt = pl.BlockSpec((TOP_K, tm), lambda i: (0, i))
    return pl.pallas_call(
        _router_kernel,
        grid=(t // tm,),
        in_specs=[pl.BlockSpec((tm, d), lambda i: (i, 0)),
                  pl.BlockSpec((ne, d), lambda i: (0, 0)),
                  pl.BlockSpec((ne, 1), lambda i: (0, 0))],
        out_specs=[kt, kt, kt, pl.BlockSpec((ne, LANES), lambda i: (0, 0))],
        out_shape=[jax.ShapeDtypeStruct((TOP_K, t), I32), jax.ShapeDtypeStruct((TOP_K, t), F32),
                   jax.ShapeDtypeStruct((TOP_K, t), I32), jax.ShapeDtypeStruct((ne, LANES), F32)],
        scratch_shapes=[pltpu.VMEM((ne, LANES), F32)],
        compiler_params=_params("arbitrary"),
        name="router",
    )(h2, rw_t, bias_col)


def _dest_kernel(start_ref, idx_ref, rank_ref, dest_ref):
    ne = start_ref.shape[0]
    idx = idx_ref[...]

    def body(e, acc):
        return jnp.where(idx == e, start_ref[e], acc)

    dest = rank_ref[...] + lax.fori_loop(0, ne, body, jnp.zeros(idx.shape, I32))
    for j in range(dest_ref.shape[0]):
        dest_ref[j] = dest[:, j * DEST_TOKENS:(j + 1) * DEST_TOKENS]


def _dest_rows(starts, idx_t, rank_t):
    t = idx_t.shape[1]
    tb = 2048
    per = tb // DEST_TOKENS
    return pl.pallas_call(
        _dest_kernel,
        grid_spec=pltpu.PrefetchScalarGridSpec(
            num_scalar_prefetch=1,
            grid=(t // tb,),
            in_specs=[pl.BlockSpec((TOP_K, tb), lambda i, s: (0, i)),
                      pl.BlockSpec((TOP_K, tb), lambda i, s: (0, i))],
            out_specs=pl.BlockSpec((per, TOP_K, DEST_TOKENS), lambda i, s: (i, 0, 0)),
        ),
        out_shape=jax.ShapeDtypeStruct((t // DEST_TOKENS, TOP_K, DEST_TOKENS), I32),
        compiler_params=_params("arbitrary"),
        name="dest_rows",
    )(starts, idx_t, rank_t)


ROW_DMA_UNROLL = 4
DEST_PER_TILE = DEST_TOKENS * TOP_K


def _idx_copy(dest_hbm, idx_smem, isem, tile, s):
    return pltpu.make_async_copy(
        dest_hbm.at[tile], idx_smem.at[pl.ds(pl.multiple_of(s * DEST_PER_TILE, DEST_PER_TILE), DEST_PER_TILE)],
        isem.at[s])


def _token_rows_at(ref, token, tr):
    return ref.at[pl.ds(pl.multiple_of(token * tr, tr), tr), :]


def _dispatch_kernel(dest_hbm, hp_ref, h_ref, w1_ref, w3_ref, w2_ref, xs_hbm, ysh_ref, idx_smem, isem, dsem, *, tr):
    i = pl.program_id(0)
    n = pl.num_programs(0)
    tm = h_ref.shape[0]
    slot = i % 2
    idx_copy = functools.partial(_idx_copy, dest_hbm, idx_smem, isem)

    @pl.when(i == 0)
    def _():
        idx_copy(0, 0).start()

    idx_copy(i, slot).wait()

    @pl.when(i + 1 < n)
    def _():
        idx_copy(i + 1, 1 - slot).start()

    base = slot * DEST_PER_TILE

    def issue(quarter):
        for t in range(quarter * tm // 4, (quarter + 1) * tm // 4):
            for k in range(TOP_K):
                d = idx_smem[base + t * TOP_K + k]
                pltpu.make_async_copy(hp_ref.at[pl.ds(t * tr, tr), :], _token_rows_at(xs_hbm, d, tr),
                                      dsem).start(priority=k % 2)

    x = h_ref[...]
    half = w2_ref.shape[1] // 2
    issue(0)
    h1 = jnp.dot(x, w1_ref[...], preferred_element_type=F32)
    issue(1)
    h3 = jnp.dot(x, w3_ref[...], preferred_element_type=F32)
    act = (_silu(h1) * h3).astype(BF16)
    issue(2)
    ysh_ref[:, :half] = jnp.dot(act, w2_ref[:, :half], preferred_element_type=F32)
    issue(3)
    ysh_ref[:, half:] = jnp.dot(act, w2_ref[:, half:], preferred_element_type=F32)
    for k in range(TOP_K):
        pltpu.make_async_copy(hp_ref, xs_hbm.at[pl.ds(0, tm * tr), :], dsem).wait()


def _dispatch(dest, h2p, h2, w1_b, w3_b, w2_b):
    t, d = h2.shape
    de = w1_b.shape[1]
    tr = _token_rows(d)
    tm = DEST_TOKENS
    resident = lambda shape: pl.BlockSpec(shape, lambda i: (0, 0), pipeline_mode=pl.Buffered(1))
    return pl.pallas_call(
        functools.partial(_dispatch_kernel, tr=tr),
        grid=(t // tm,),
        in_specs=[pl.BlockSpec(memory_space=pl.ANY), pl.BlockSpec((tm * tr, LANES), lambda i: (i, 0)),
                  pl.BlockSpec((tm, d), lambda i: (i, 0)),
                  resident((d, de)), resident((d, de)), resident((de, d))],
        out_specs=[pl.BlockSpec(memory_space=pl.ANY), pl.BlockSpec((tm, d), lambda i: (i, 0))],
        out_shape=[jax.ShapeDtypeStruct((t * TOP_K * tr, LANES), U32), jax.ShapeDtypeStruct((t, d), F32)],
        scratch_shapes=[pltpu.SMEM((2 * DEST_PER_TILE,), I32), pltpu.SemaphoreType.DMA((2,)),
                        pltpu.SemaphoreType.DMA(())],
        compiler_params=_params("arbitrary"),
        name="dispatch_shared",
    )(dest, h2p, h2, w1_b, w3_b, w2_b)


def _experts_kernel(e_ref, b_ref, lo_ref, hi_ref, nxt_ref, half_ref, n_ref, xs_ref, w1_hbm, w3_hbm, w2_hbm, ys_ref,
                    w1s, w3s, w2s, w1b, w3b, w2b, xb, yp, wsem):
    w = pl.program_id(0)
    prev = jnp.maximum(w - 1, 0)

    def fetch(e):
        return (pltpu.make_async_copy(w1_hbm.at[e], w1s, wsem.at[0]),
                pltpu.make_async_copy(w3_hbm.at[e], w3s, wsem.at[1]),
                pltpu.make_async_copy(w2_hbm.at[e], w2s, wsem.at[2]))

    @pl.when(w < n_ref[0])
    def _():
        @pl.when(w == 0)
        def _():
            for cp in fetch(e_ref[0]):
                cp.start()

        @pl.when(jnp.logical_or(w == 0, e_ref[w] != e_ref[prev]))
        def _():
            for cp in fetch(e_ref[w]):
                cp.wait()
            w1b[...] = w1s[...].astype(BF16)
            w3b[...] = w3s[...].astype(BF16)
            w2b[...] = w2s[...].astype(BF16)

            @pl.when(nxt_ref[w] >= 0)
            def _():
                for cp in fetch(nxt_ref[w]):
                    cp.start()

        d = xb.shape[1]
        tr = _token_rows(d)
        new_block = jnp.logical_or(w == 0, b_ref[w] != b_ref[prev])

        @pl.when(new_block)
        def _():
            for s in range(tr):
                lo, hi = _unpack_slab(xs_ref, s, EXPERT_ROWS, tr)
                xb[:, s * LANES:(s + 1) * LANES] = lo.astype(BF16)
                xb[:, d // 2 + s * LANES:d // 2 + (s + 1) * LANES] = hi.astype(BF16)
            ys_ref[...] = jnp.zeros(ys_ref.shape, U32)

        first = (lo_ref[w] - b_ref[w] * EXPERT_ROWS) * tr
        last = (hi_ref[w] - b_ref[w] * EXPERT_ROWS) * tr

        def run(row0, nrows):
            x = xb[pl.ds(row0, nrows), :]
            h1 = jnp.dot(x, w1b[...], preferred_element_type=F32)
            h3 = jnp.dot(x, w3b[...], preferred_element_type=F32)
            act = (_silu(h1) * h3).astype(BF16)
            span = pl.ds(row0 * tr, nrows * tr)
            _pack_rows(jnp.dot(act, w2b[...], preferred_element_type=F32), yp.at[span, :])
            prow = row0 * tr + lax.broadcasted_iota(I32, (nrows * tr, 1), 0)
            mine = jnp.logical_and(prow >= first, prow < last)
            ys_ref[span, :] = jnp.where(mine, yp[span, :], ys_ref[span, :])

        half_rows = EXPERT_ROWS // 2

        @pl.when(half_ref[w] == 0)
        def _():
            run(0, EXPERT_ROWS)

        @pl.when(half_ref[w] != 0)
        def _():
            run(pl.multiple_of((half_ref[w] - 1) * half_rows, half_rows), half_rows)


def _experts(item_expert, item_block, item_lo, item_hi, item_next, item_half, n_items, xs, w1, w3, w2):
    _, d, de = w1.shape
    tr = _token_rows(d)
    rows = lambda w, e, b, lo, hi, nx, hf, n: (b[w], 0)
    hbm = pl.BlockSpec(memory_space=pl.ANY)
    return pl.pallas_call(
        _experts_kernel,
        grid_spec=pltpu.PrefetchScalarGridSpec(
            num_scalar_prefetch=7,
            grid=(item_expert.shape[0],),
            in_specs=[pl.BlockSpec((EXPERT_ROWS * tr, LANES), rows), hbm, hbm, hbm],
            out_specs=pl.BlockSpec((EXPERT_ROWS * tr, LANES), rows),
            scratch_shapes=[pltpu.VMEM((d, de), F32), pltpu.VMEM((d, de), F32), pltpu.VMEM((de, d), F32),
                            pltpu.VMEM((d, de), BF16), pltpu.VMEM((d, de), BF16), pltpu.VMEM((de, d), BF16),
                            pltpu.VMEM((EXPERT_ROWS, d), BF16), pltpu.VMEM((EXPERT_ROWS * tr, LANES), U32),
                            pltpu.SemaphoreType.DMA((3,))],
        ),
        out_shape=jax.ShapeDtypeStruct(xs.shape, U32),
        compiler_params=_params("arbitrary"),
        name="experts",
    )(item_expert, item_block, item_lo, item_hi, item_next, item_half, n_items, xs, w1, w3, w2)


def _combine_kernel(dest_hbm, ys_hbm, x1_ref, ysh_ref, wt_ref, gate_ref, fg_ref, o_ref,
                    rows0, rows1, idx_smem, isem, gsem):
    i = pl.program_id(0)
    n = pl.num_programs(0)
    tm, d = x1_ref.shape
    tr = _token_rows(d)
    slot = i % 2
    idx_copy = functools.partial(_idx_copy, dest_hbm, idx_smem, isem)

    def gather(buf, s, t, k):
        src = idx_smem[s * DEST_PER_TILE + t * TOP_K + k]
        return pltpu.make_async_copy(_token_rows_at(ys_hbm, src, tr), buf.at[k, pl.ds(t * tr, tr), :], gsem.at[s])

    def wait_gathers(buf, s):
        for k in range(TOP_K):
            pltpu.make_async_copy(ys_hbm.at[pl.ds(0, tm * tr), :], buf.at[k], gsem.at[s]).wait()

    @pl.when(i == 0)
    def _():
        idx_copy(0, 0).start()
        idx_copy(0, 0).wait()

        def body(c, carry):
            for u in range(ROW_DMA_UNROLL):
                for k in range(TOP_K):
                    gather(rows0, 0, c * ROW_DMA_UNROLL + u, k).start(priority=k % 2)
            return carry

        lax.fori_loop(0, tm // ROW_DMA_UNROLL, body, 0)

        @pl.when(n > 1)
        def _():
            idx_copy(1, 1).start()

    @pl.when(i + 1 < n)
    def _():
        idx_copy(i + 1, 1 - slot).wait()

    @pl.when(i + 2 < n)
    def _():
        idx_copy(i + 2, slot).start()

    def step(cur, cur_s, nxt, nxt_s):
        wait_gathers(cur, cur_s)
        w = wt_ref[...]
        ssq = jnp.zeros((tm, 1), F32)
        for s in range(tr):
            for t in range(s * tm // tr, (s + 1) * tm // tr):
                for k in range(TOP_K):
                    gather(nxt, nxt_s, t, k).start(priority=k % 2)
            cols = (slice(s * LANES, (s + 1) * LANES), slice(d // 2 + s * LANES, d // 2 + (s + 1) * LANES))
            acc = [ysh_ref[:, cs] for cs in cols]
            for k in range(TOP_K):
                halves = _unpack_slab(cur.at[k], s, tm, tr)
                acc = [a + v * w[:, k:k + 1] for a, v in zip(acc, halves)]
            for cs, a in zip(cols, acc):
                x2 = x1_ref[:, cs] + gate_ref[:, cs] * a
                o_ref[:, cs] = x2
                ssq = ssq + jnp.sum(x2 * x2, axis=-1, keepdims=True)
        o_ref[...] = o_ref[...] * lax.rsqrt(ssq * (1.0 / d) + NORM_EPS) * fg_ref[...]

        @pl.when(i + 1 == n)
        def _():
            wait_gathers(nxt, nxt_s)

    @pl.when(slot == 0)
    def _():
        step(rows0, 0, rows1, 1)

    @pl.when(slot == 1)
    def _():
        step(rows1, 1, rows0, 0)


def _combine(dest, ys, x1, ysh, w_tok, gate2, final_g, seq):
    t, d = x1.shape
    tr = _token_rows(d)
    tm = DEST_TOKENS
    assert t // tm >= 2, "the gather ring keeps two token tiles in flight"
    per_b = seq // tm
    row = pl.BlockSpec((tm, d), lambda i: (i, 0))
    return pl.pallas_call(
        _combine_kernel,
        grid=(t // tm,),
        in_specs=[pl.BlockSpec(memory_space=pl.ANY), pl.BlockSpec(memory_space=pl.ANY), row, row,
                  pl.BlockSpec((tm, TOP_K), lambda i: (i, 0)),
                  pl.BlockSpec((None, 1, d), lambda i: (i // per_b, 0, 0)),
                  pl.BlockSpec((1, d), lambda i: (0, 0))],
        out_specs=row,
        out_shape=jax.ShapeDtypeStruct((t, d), F32),
        scratch_shapes=[pltpu.VMEM((TOP_K, tm * tr, LANES), U32), pltpu.VMEM((TOP_K, tm * tr, LANES), U32),
                        pltpu.SMEM((2 * DEST_PER_TILE,), I32),
                        pltpu.SemaphoreType.DMA((2,)), pltpu.SemaphoreType.DMA((2,))],
        compiler_params=_params("arbitrary"),
        name="combine",
    )(dest, ys, x1, ysh, w_tok, gate2, final_g)


def _mixer(x2, mod6, cos_t, sin_t, bsz, seq, p):
    t, d = x2.shape
    shift1, scale1, gate1, shift2, scale2, _ = mod6
    d_rnn = p["conv_w"].shape[1]
    att_width = len(DILATION_GROUPS) * GROUP_COLS
    q_col = 2 * d_rnn
    gate_col = 2 * d_rnn + 3 * att_width

    def qkv_weights(g):
        parts = [p["w_in"][:, q_col + j * att_width + g * GROUP_COLS:q_col + j * att_width + (g + 1) * GROUP_COLS]
                 for j in range(3)]
        return jnp.concatenate(parts, axis=1).astype(BF16)

    h1 = _norm_mod(x2, p["norm1_g"].reshape(1, d), shift1, scale1, seq)
    xr = _proj_act(h1, p["w_in"][:, :d_rnn].astype(BF16), None, "proj_rnn_x")
    gr = _proj_act(h1, p["w_in"][:, d_rnn:2 * d_rnn].astype(BF16), _gelu_tanh, "proj_rnn_gate")
    gates = _proj_act(h1, p["w_in"][:, gate_col:].astype(BF16), _sigmoid_tanh, "proj_gates")
    ya = _rglru(xr, gr, p["conv_w"], p["conv_b"].reshape(1, d_rnn),
                (0.5 * p["rg_wa"]).astype(BF16), 0.5 * p["rg_ba"].reshape(1, d_rnn),
                (0.5 * p["rg_wi"]).astype(BF16), 0.5 * p["rg_bi"].reshape(1, d_rnn),
                p["rg_lambda"].reshape(1, d_rnn), bsz, seq, d_rnn)

    outs, lses = [], []
    for g, (window, dilation) in enumerate(DILATION_GROUPS):
        qkv = _proj_qkv(h1, qkv_weights(g), cos_t, sin_t, g, dilation, bsz, seq)
        o, l = _attention_group(qkv, g, window, dilation)
        outs.append(o)
        lses.append(l)

    merged = _merge(ya, outs, lses, gates, p["w_proj_rnn"].astype(BF16), p["w_proj_attn"].astype(BF16), seq)
    return _out_proj(merged, p["w_out"].astype(BF16), x2, gate1, p["norm2_g"].reshape(1, d),
                     shift2, scale2, seq)


def _moe(h2, h2p, p):
    t, d = h2.shape
    ne = p["router_w"].shape[1]
    idx_t, w_t, rank_t, cnt = _router(h2, p["router_w"].T.astype(BF16), p["router_bias"].reshape(ne, 1))

    counts = cnt[:, 0].astype(I32)
    ends = jnp.cumsum(counts).astype(I32)
    starts = ends - counts
    n_rows = t * TOP_K
    first_blk = starts // EXPERT_ROWS
    n_blk_e = jnp.where(counts > 0, (ends - 1) // EXPERT_ROWS - first_blk + 1, 0)
    item_end = jnp.cumsum(n_blk_e).astype(I32)
    item_start = item_end - n_blk_e
    n_items = item_end[-1]
    max_items = n_rows // EXPERT_ROWS + ne
    w = jnp.minimum(jnp.arange(max_items, dtype=I32), n_items - 1)
    owner = lambda i: jnp.minimum(jnp.sum((item_end[None, :] <= i[:, None]).astype(I32), axis=1), ne - 1)
    item_expert = owner(w)
    onehot = item_expert[:, None] == jnp.arange(ne, dtype=I32)[None, :]
    pick = lambda table: jnp.sum(jnp.where(onehot, table[None, :], 0), axis=1).astype(I32)
    item_block = pick(first_blk) + (w - pick(item_start))
    after = pick(item_end)
    item_next = jnp.where(after < n_items, owner(after), -1).astype(I32)

    dest = _dest_rows(starts, idx_t, rank_t)
    dest = jnp.transpose(dest, (0, 2, 1)).reshape(t // DEST_TOKENS, DEST_PER_TILE)
    xs, ysh = _dispatch(dest, h2p, h2, p["sh_w1"].astype(BF16), p["sh_w3"].astype(BF16), p["sh_w2"].astype(BF16))
    item_lo, item_hi = pick(starts), pick(ends)
    in_lo = jnp.maximum(item_lo, item_block * EXPERT_ROWS) - item_block * EXPERT_ROWS
    in_hi = jnp.minimum(item_hi, (item_block + 1) * EXPERT_ROWS) - item_block * EXPERT_ROWS
    item_half = jnp.where(in_hi <= EXPERT_ROWS // 2, 1, jnp.where(in_lo >= EXPERT_ROWS // 2, 2, 0)).astype(I32)
    ys = _experts(item_expert, item_block, item_lo, item_hi, item_next, item_half,
                  n_items.reshape(1), xs, p["exp_w1"], p["exp_w3"], p["exp_w2"])
    return dest, ys, ysh, w_t.T


def kernel(x, c, positions, ada_w, ada_b, norm1_g, w_in, conv_w, conv_b, rg_wa, rg_ba, rg_wi, rg_bi, rg_lambda, w_proj_rnn, w_proj_attn, w_out, norm2_g, router_w, router_bias, exp_w1, exp_w3, exp_w2, sh_w1, sh_w3, sh_w2, final_g):
    bsz, seq, d = x.shape
    assert ada_w.shape[0] == 1, "the fused final norm assumes a single layer"
    t = bsz * seq
    x2 = x.reshape(t, d)
    first = lambda a: a.reshape(a.shape[1:])

    half = HEAD_DIM // 2
    inv_freq = ROPE_THETA ** (-jnp.arange(half, dtype=F32) * 2.0 / HEAD_DIM)
    freq = jnp.concatenate([inv_freq, inv_freq]).reshape(1, HEAD_DIM)
    sign = jnp.concatenate([-jnp.ones((half,), F32), jnp.ones((half,), F32)]).reshape(1, HEAD_DIM)
    cos_t, sin_t = _rope_tables(positions.reshape(t, 1), freq, sign)

    c_pad = jnp.zeros((SUBLANES, d), F32).at[:bsz].set(c)
    mod = _ada_mod(c_pad, first(ada_w), ada_b.reshape(1, -1))
    mod6 = tuple(mod[:bsz, k * d:(k + 1) * d].reshape(bsz, 1, d) for k in range(6))

    p = dict(norm1_g=first(norm1_g), w_in=first(w_in), conv_w=first(conv_w), conv_b=first(conv_b),
             rg_wa=first(rg_wa), rg_ba=first(rg_ba), rg_wi=first(rg_wi), rg_bi=first(rg_bi),
             rg_lambda=first(rg_lambda), w_proj_rnn=first(w_proj_rnn), w_proj_attn=first(w_proj_attn),
             w_out=first(w_out), norm2_g=first(norm2_g), router_w=first(router_w),
             router_bias=first(router_bias), exp_w1=first(exp_w1), exp_w3=first(exp_w3),
             exp_w2=first(exp_w2), sh_w1=first(sh_w1), sh_w3=first(sh_w3), sh_w2=first(sh_w2))
    x1, h2, h2p = _mixer(x2, mod6, cos_t, sin_t, bsz, seq, p)
    dest, ys, ysh, w_tok = _moe(h2, h2p, p)
    out = _combine(dest, ys, x1, ysh, w_tok, mod6[5], final_g.reshape(1, d), seq)
    return out.reshape(bsz, seq, d)
```
